```python
import jax, jax.numpy as jnp
from jax import lax
import numpy as np

D_MODEL = 1024
BATCH = 16
SEQ = 4096
DEPTH = 4

N_MIXERS = 4
EPS = 1e-6
GLA_HEADS = 4
GLA_DK = D_MODEL // 2 // GLA_HEADS
GLA_DV = D_MODEL // GLA_HEADS
GLA_GATE_RANK = 16
GLA_GATE_NORM = 16.0
GLA_CHUNK = 64
CONV_WIDTH = 31
SGU_CHUNK = 128
SGU_GROUPS = 8
SGU_DIM = D_MODEL
HGRN_EXPAND = 128
HGRN_HEADS = D_MODEL // HGRN_EXPAND
HGRN_CHUNK = 64
FFN_HIDDEN = 2816
FFN_CONV_WIDTH = 3

kernel_name = "hybrid_gla_conformer_sgu_hgrn2_trunk"


def n_uses(m):
    return (DEPTH - m + N_MIXERS - 1) // N_MIXERS


def rms_norm(x, g):
    x32 = x.astype(jnp.float32)
    y = x32 * lax.rsqrt(jnp.mean(x32 * x32, axis=-1, keepdims=True) + EPS)
    return (y * g.astype(jnp.float32)).astype(x.dtype)


def layer_norm(x, g, b):
    x32 = x.astype(jnp.float32)
    mu = jnp.mean(x32, axis=-1, keepdims=True)
    xc = x32 - mu
    y = xc * lax.rsqrt(jnp.mean(xc * xc, axis=-1, keepdims=True) + EPS)
    return (y * g.astype(jnp.float32) + b.astype(jnp.float32)).astype(x.dtype)


def causal_depthwise_conv(x, w):
    k_width, ch = w.shape
    return lax.conv_general_dilated(
        x, w.astype(x.dtype)[:, None, :], window_strides=(1,),
        padding=[(k_width - 1, 0)], dimension_numbers=("NWC", "WIO", "NWC"),
        feature_group_count=ch)


def chunk_gated_linear_attention(q, k, v, log_g, chunk):
    f32 = jnp.float32
    bsz, seq, heads, dk = q.shape
    dv = v.shape[-1]
    n = seq // chunk
    qc = q.astype(f32).reshape(bsz, n, chunk, heads, dk)
    kc = k.astype(f32).reshape(bsz, n, chunk, heads, dk)
    vc = v.astype(f32).reshape(bsz, n, chunk, heads, dv)
    b = jnp.cumsum(log_g.astype(f32).reshape(bsz, n, chunk, heads, dk), axis=2)
    b_mid = b[:, :, chunk // 2:chunk // 2 + 1]
    b_last = b[:, :, -1:]
    scores = jnp.einsum('bnihd,bnjhd->bnhij', qc * jnp.exp(b - b_mid), kc * jnp.exp(b_mid - b))
    causal = jnp.tril(jnp.ones((chunk, chunk), dtype=bool))
    scores = jnp.where(causal, scores, 0.0)
    o_intra = jnp.einsum('bnhij,bnjhe->bnihe', scores, vc)
    q_dec = qc * jnp.exp(b)
    k_dec = kc * jnp.exp(b_last - b)
    g_chunk = jnp.exp(b_last[:, :, 0])

    def step(state, xs):
        q_n, k_n, v_n, g_n = xs
        o_n = jnp.einsum('bihd,bhde->bihe', q_n, state)
        state = g_n[..., None] * state + jnp.einsum('bjhd,bjhe->bhde', k_n, v_n)
        return state, o_n

    s0 = jnp.zeros((bsz, heads, dk, dv), f32)
    _, o_inter = lax.scan(step, s0, (jnp.moveaxis(q_dec, 1, 0), jnp.moveaxis(k_dec, 1, 0),
                                     jnp.moveaxis(vc, 1, 0), jnp.moveaxis(g_chunk, 1, 0)))
    o = o_intra + jnp.moveaxis(o_inter, 0, 1)
    return o.reshape(bsz, seq, heads, dv)


def gla_mixer(h, w_in, w_g2, b_g2, norm_g, w_out):
    bsz, seq, _ = h.shape
    dk_t, dv_t = GLA_HEADS * GLA_DK, GLA_HEADS * GLA_DV
    proj = h @ w_in
    q, k, v, r, g_lr = jnp.split(proj, [dk_t, 2 * dk_t, 2 * dk_t + dv_t, 2 * dk_t + 2 * dv_t], axis=-1)
    log_g = jax.nn.log_sigmoid((g_lr @ w_g2 + b_g2).astype(jnp.float32)) / GLA_GATE_NORM
    o = chunk_gated_linear_attention(
        (q * GLA_DK ** -0.5).reshape(bsz, seq, GLA_HEADS, GLA_DK),
        k.reshape(bsz, seq, GLA_HEADS, GLA_DK),
        v.reshape(bsz, seq, GLA_HEADS, GLA_DV),
        log_g.reshape(bsz, seq, GLA_HEADS, GLA_DK), GLA_CHUNK)
    o = rms_norm(o, norm_g).reshape(bsz, seq, dv_t) * jax.nn.silu(r.astype(jnp.float32))
    return o.astype(h.dtype) @ w_out


def conformer_conv_mixer(h, w_in, b_in, w_dw, b_dw, ln_g, ln_b, w_out, b_out):
    a, gate = jnp.split(h @ w_in + b_in, 2, axis=-1)
    y = a * jax.nn.sigmoid(gate)
    y = causal_depthwise_conv(y, w_dw) + b_dw
    y = jax.nn.silu(layer_norm(y, ln_g, ln_b))
    return y @ w_out + b_out


def sgu_mixer(h, w_in, b_in, ln_g, ln_b, w_s, b_s, w_out, b_out):
    bsz, seq, _ = h.shape
    u, v = jnp.split(jax.nn.gelu(h @ w_in + b_in), 2, axis=-1)
    v = layer_norm(v, ln_g, ln_b)
    n = seq // SGU_CHUNK
    v = v.reshape(bsz, n, SGU_CHUNK, SGU_GROUPS, SGU_DIM // SGU_GROUPS)
    causal = jnp.tril(jnp.ones((SGU_CHUNK, SGU_CHUNK), dtype=bool))
    w_causal = jnp.where(causal, w_s, 0.0).astype(v.dtype)
    s = jnp.einsum('gij,bnjgc->bnigc', w_causal, v) + b_s.T.astype(v.dtype)[:, :, None]
    return (u * s.reshape(bsz, seq, SGU_DIM)) @ w_out + b_out


def hgrn2_mixer(h, w_in, lb, norm_g, w_out):
    bsz, seq, _ = h.shape
    q, f, i, g = jnp.split(h @ w_in, 4, axis=-1)
    lb = lb.astype(jnp.float32)
    log_f = jnp.logaddexp(jnp.log(lb), jnp.log1p(-lb) + jax.nn.log_sigmoid(f.astype(jnp.float32)))
    k = -jnp.expm1(log_f)
    shp = (bsz, seq, HGRN_HEADS, HGRN_EXPAND)
    o = chunk_gated_linear_attention(jax.nn.silu(q).reshape(shp), k.reshape(shp),
                                     i.reshape(shp), log_f.reshape(shp), HGRN_CHUNK)
    o = rms_norm(o, norm_g).reshape(bsz, seq, D_MODEL) * jax.nn.silu(g.astype(jnp.float32))
    return o.astype(h.dtype) @ w_out


def conv_ffn(h, w_up, w_dw, w_down):
    z = causal_depthwise_conv(h @ w_up, w_dw)
    gate, val = jnp.split(z, 2, axis=-1)
    return (jax.nn.silu(gate) * val) @ w_down


def setup_inputs(seed: int = 0) -> dict:
    key = jax.random.key(seed)
    ks = iter(jax.random.split(key, 48))
    D = D_MODEL
    na, nb, nc, nd = n_uses(0), n_uses(1), n_uses(2), n_uses(3)

    def nrm(shape, scale):
        return scale * jax.random.normal(next(ks), shape, jnp.float32)

    def gain(shape):
        return 1.0 + nrm(shape, 0.02)

    gla_in = 2 * GLA_HEADS * GLA_DK + 2 * GLA_HEADS * GLA_DV + GLA_GATE_RANK
    return {
        "x": nrm((BATCH, SEQ, D), 1.0),
        "norm_mix": gain((DEPTH, D)),
        "norm_ffn": gain((DEPTH, D)),
        "norm_final": gain((D,)),
        "gla_w_in": nrm((na, D, gla_in), D ** -0.5),
        "gla_w_g2": nrm((na, GLA_GATE_RANK, GLA_HEADS * GLA_DK), GLA_GATE_RANK ** -0.5),
        "gla_b_g2": nrm((na, GLA_HEADS * GLA_DK), 0.01),
        "gla_norm": gain((na, GLA_DV)),
        "gla_w_out": nrm((na, GLA_HEADS * GLA_DV, D), (GLA_HEADS * GLA_DV) ** -0.5),
        "cv_w_in": nrm((nb, D, 2 * D), D ** -0.5),
        "cv_b_in": nrm((nb, 2 * D), 0.01),
        "cv_w_dw": nrm((nb, CONV_WIDTH, D), CONV_WIDTH ** -0.5),
        "cv_b_dw": nrm((nb, D), 0.01),
        "cv_ln_g": gain((nb, D)),
        "cv_ln_b": nrm((nb, D), 0.01),
        "cv_w_out": nrm((nb, D, D), D ** -0.5),
        "cv_b_out": nrm((nb, D), 0.01),
        "sg_w_in": nrm((nc, D, 2 * SGU_DIM), D ** -0.5),
        "sg_b_in": nrm((nc, 2 * SGU_DIM), 0.01),
        "sg_ln_g": gain((nc, SGU_DIM)),
        "sg_ln_b": nrm((nc, SGU_DIM), 0.01),
        "sg_w_s": nrm((nc, SGU_GROUPS, SGU_CHUNK, SGU_CHUNK), SGU_CHUNK ** -0.5),
        "sg_b_s": gain((nc, SGU_GROUPS, SGU_CHUNK)),
        "sg_w_out": nrm((nc, SGU_DIM, D), SGU_DIM ** -0.5),
        "sg_b_out": nrm((nc, D), 0.01),
        "hg_w_in": nrm((nd, D, 4 * D), D ** -0.5),
        "hg_lb_table": nrm((DEPTH, D), 0.1),
        "hg_norm": gain((nd, HGRN_EXPAND)),
        "hg_w_out": nrm((nd, D, D), D ** -0.5),
        "ffn_w_up": nrm((DEPTH, D, 2 * FFN_HIDDEN), D ** -0.5),
        "ffn_w_dw": nrm((DEPTH, FFN_CONV_WIDTH, 2 * FFN_HIDDEN), FFN_CONV_WIDTH ** -0.5),
        "ffn_w_down": nrm((DEPTH, FFN_HIDDEN, D), FFN_HIDDEN ** -0.5),
    }


def reference(x, norm_mix, norm_ffn, norm_final,
              gla_w_in, gla_w_g2, gla_b_g2, gla_norm, gla_w_out,
              cv_w_in, cv_b_in, cv_w_dw, cv_b_dw, cv_ln_g, cv_ln_b, cv_w_out, cv_b_out,
              sg_w_in, sg_b_in, sg_ln_g, sg_ln_b, sg_w_s, sg_b_s, sg_w_out, sg_b_out,
              hg_w_in, hg_lb_table, hg_norm, hg_w_out,
              ffn_w_up, ffn_w_dw, ffn_w_down):
    lb_cum = jnp.cumsum(jax.nn.softmax(hg_lb_table.astype(jnp.float32), axis=0), axis=0)
    lower_bounds = lb_cum - lb_cum[0]
    for layer in range(DEPTH):
        m, j = layer % N_MIXERS, layer // N_MIXERS
        h = rms_norm(x, norm_mix[layer])
        if m == 0:
            y = gla_mixer(h, gla_w_in[j], gla_w_g2[j], gla_b_g2[j], gla_norm[j], gla_w_out[j])
        elif m == 1:
            y = conformer_conv_mixer(h, cv_w_in[j], cv_b_in[j], cv_w_dw[j], cv_b_dw[j],
                                     cv_ln_g[j], cv_ln_b[j], cv_w_out[j], cv_b_out[j])
        elif m == 2:
            y = sgu_mixer(h, sg_w_in[j], sg_b_in[j], sg_ln_g[j], sg_ln_b[j],
                          sg_w_s[j], sg_b_s[j], sg_w_out[j], sg_b_out[j])
        else:
            y = hgrn2_mixer(h, hg_w_in[j], lower_bounds[layer], hg_norm[j], hg_w_out[j])
        x = x + y
        x = x + conv_ffn(rms_norm(x, norm_ffn[layer]), ffn_w_up[layer], ffn_w_dw[layer], ffn_w_down[layer])
    return rms_norm(x, norm_final)
```

```python
import functools

import jax
import jax.numpy as jnp
from jax import lax
from jax.experimental import pallas as pl
from jax.experimental.pallas import tpu as pltpu

F32 = jnp.float32
BF16 = jnp.bfloat16

EPS = 1e-6
N_MIXERS = 4
GLA_HEADS = 4
GLA_GATE_NORM = 16.0
GLA_CHUNK = 64
SGU_CHUNK = 128
SGU_GROUPS = 8
HGRN_EXPAND = 128
HGRN_CHUNK = 64

V7X_LANES = 128
V7X_SUBLANES = 8
V7X_VMEM_LIMIT_BYTES = 56 * 1024 * 1024

MIXER_TILE = 256
FFN_TILE = 512
FFN_HIDDEN_BLOCK = 256


def _dot(a, b):
    return jnp.dot(a.astype(BF16), b.astype(BF16), preferred_element_type=F32)


def _dot_nt(a, b):
    return lax.dot_general(a.astype(BF16), b.astype(BF16),
                           (((1,), (1,)), ((), ())), preferred_element_type=F32)


def _dot_tn(a, b):
    return lax.dot_general(a.astype(BF16), b.astype(BF16),
                           (((0,), (0,)), ((), ())), preferred_element_type=F32)


def _rms_norm(x, g):
    return x * lax.rsqrt(jnp.mean(x * x, axis=-1, keepdims=True) + EPS) * g


def _layer_norm(x, g, b):
    mu = jnp.mean(x, axis=-1, keepdims=True)
    xc = x - mu
    return xc * lax.rsqrt(jnp.mean(xc * xc, axis=-1, keepdims=True) + EPS) * g + b


def _sigmoid(x):
    return 1.0 / (1.0 + jnp.exp(-x))


def _silu(x):
    return x * _sigmoid(x)


def _log_sigmoid(x):
    return jnp.minimum(x, 0.0) - jnp.log1p(jnp.exp(-jnp.abs(x)))


def _gelu_tanh(x):
    c = 0.7978845608028654
    return 0.5 * x * (1.0 + jnp.tanh(c * (x + 0.044715 * (x * x * x))))


def _chunk_cumsum(x, chunk):
    rows = x.shape[0]
    pos = lax.broadcasted_iota(jnp.int32, (rows, 1), 0) & (chunk - 1)
    shift = 1
    while shift < chunk:
        x = x + jnp.where(pos >= shift, pltpu.roll(x, shift, axis=0), 0.0)
        shift *= 2
    return x


def _row_to_col(row):
    n = row.shape[1]
    eye = (lax.broadcasted_iota(jnp.int32, (n, n), 0)
           == lax.broadcasted_iota(jnp.int32, (n, n), 1))
    return jnp.sum(jnp.where(eye, jnp.broadcast_to(row, (n, n)), 0.0),
                   axis=1, keepdims=True)


def _chunked_linear_attention(q, k, v, log_g, state_ref, o_ref, *, heads, dk, dv, chunk):
    rows = q.shape[0]
    n_chunks = rows // chunk
    b = _chunk_cumsum(log_g, chunk)
    causal = (lax.broadcasted_iota(jnp.int32, (chunk, chunk), 0)
              >= lax.broadcasted_iota(jnp.int32, (chunk, chunk), 1))
    for n in range(n_chunks):
        r0 = n * chunk
        b_n = b[r0:r0 + chunk]
        q_n = q[r0:r0 + chunk]
        k_n = k[r0:r0 + chunk]
        b_mid = b_n[chunk // 2:chunk // 2 + 1]
        b_last = b_n[chunk - 1:chunk]
        q_in = (q_n * jnp.exp(b_n - b_mid)).astype(BF16)
        k_in = (k_n * jnp.exp(b_mid - b_n)).astype(BF16)
        q_dec = (q_n * jnp.exp(b_n)).astype(BF16)
        k_dec = (k_n * jnp.exp(b_last - b_n)).astype(BF16)
        g_chunk = jnp.exp(b_last)
        for h in range(heads):
            ks = slice(h * dk, (h + 1) * dk)
            vs = slice(h * dv, (h + 1) * dv)
            v_nh = v[r0:r0 + chunk, vs].astype(BF16)
            scores = jnp.where(causal, _dot_nt(q_in[:, ks], k_in[:, ks]), 0.0)
            state = state_ref[h]
            o_nh = _dot(scores, v_nh) + _dot(q_dec[:, ks], state)
            o_ref[r0:r0 + chunk, vs] = o_nh
            state_ref[h] = (_row_to_col(g_chunk[:, ks]) * state
                            + _dot_tn(k_dec[:, ks], v_nh))


def _gla_kernel(x_ref, ng_ref, w_qkvr_ref, w_glr_ref, w_g2_ref, b_g2_ref, hn_ref,
                w_out_ref, o_ref, state_ref, att_ref, *, heads, dk, dv, chunk):
    @pl.when(pl.program_id(1) == 0)
    def _():
        state_ref[...] = jnp.zeros_like(state_ref)

    x = x_ref[0]
    h = _rms_norm(x, ng_ref[...]).astype(BF16)
    dk_t, dv_t = heads * dk, heads * dv
    proj = _dot(h, w_qkvr_ref[...])
    q = proj[:, :dk_t] * (dk ** -0.5)
    k = proj[:, dk_t:2 * dk_t]
    v = proj[:, 2 * dk_t:2 * dk_t + dv_t]
    r = proj[:, 2 * dk_t + dv_t:]
    g_lr = _dot(h, w_glr_ref[...])
    log_g = _log_sigmoid(_dot(g_lr, w_g2_ref[...]) + b_g2_ref[...]) / GLA_GATE_NORM
    _chunked_linear_attention(q, k, v, log_g, state_ref, att_ref,
                              heads=heads, dk=dk, dv=dv, chunk=chunk)
    gated = []
    for hd in range(heads):
        vs = slice(hd * dv, (hd + 1) * dv)
        gated.append((_rms_norm(att_ref[:, vs], hn_ref[...]) * _silu(r[:, vs])).astype(BF16))
    o_ref[0] = x + _dot(jnp.concatenate(gated, axis=1), w_out_ref[...])


def _conformer_kernel(x_ref, ng_ref, w_in_ref, b_in_ref, w_dw_ref, b_dw_ref, ln_g_ref,
                      ln_b_ref, w_out_ref, b_out_ref, o_ref, y_ref, *, width, halo):
    rows = x_ref.shape[1]
    d = x_ref.shape[2]

    @pl.when(pl.program_id(1) == 0)
    def _():
        y_ref[0:halo, :] = jnp.zeros((halo, d), F32)

    @pl.when(pl.program_id(1) != 0)
    def _():
        y_ref[0:halo, :] = y_ref[rows:rows + halo, :]

    x = x_ref[0]
    h = _rms_norm(x, ng_ref[...]).astype(BF16)
    ag = _dot(h, w_in_ref[...]) + b_in_ref[...]
    y_ref[halo:halo + rows, :] = ag[:, :d] * _sigmoid(ag[:, d:])
    acc = jnp.zeros((rows, d), F32) + b_dw_ref[...]
    for tap in range(width):
        start = halo - (width - 1) + tap
        acc = acc + w_dw_ref[tap:tap + 1, :] * y_ref[start:start + rows, :]
    z = _silu(_layer_norm(acc, ln_g_ref[...], ln_b_ref[...]))
    o_ref[0] = x + _dot(z, w_out_ref[...]) + b_out_ref[...]


def _sgu_kernel(x_ref, ng_ref, w_in_ref, b_in_ref, ln_g_ref, ln_b_ref, w_s_ref, b_s_ref,
                w_out_ref, b_out_ref, o_ref, gated_ref, *, chunk, groups):
    rows = x_ref.shape[1]
    d = x_ref.shape[2]
    gw = d // groups
    x = x_ref[0]
    h = _rms_norm(x, ng_ref[...]).astype(BF16)
    uv = _gelu_tanh(_dot(h, w_in_ref[...]) + b_in_ref[...])
    u = uv[:, :d]
    v = _layer_norm(uv[:, d:], ln_g_ref[...], ln_b_ref[...]).astype(BF16)
    causal = (lax.broadcasted_iota(jnp.int32, (chunk, chunk), 0)
              >= lax.broadcasted_iota(jnp.int32, (chunk, chunk), 1))
    for g in range(groups):
        cs = slice(g * gw, (g + 1) * gw)
        w_causal = jnp.where(causal, w_s_ref[g], 0.0).astype(BF16)
        for n in range(rows // chunk):
            rs = slice(n * chunk, (n + 1) * chunk)
            s = _dot(w_causal, v[rs, cs]) + b_s_ref[:, cs]
            gated_ref[rs, cs] = (u[rs, cs] * s).astype(BF16)
    o_ref[0] = x + _dot(gated_ref[...], w_out_ref[...]) + b_out_ref[...]


def _hgrn2_kernel(x_ref, ng_ref, w_in_ref, lb_table_ref, hn_ref, w_out_ref, o_ref,
                  state_ref, att_ref, *, layer, heads, dk, chunk):
    @pl.when(pl.program_id(1) == 0)
    def _():
        state_ref[...] = jnp.zeros_like(state_ref)

    d = x_ref.shape[2]
    table = lb_table_ref[...]
    e = jnp.exp(table - jnp.max(table, axis=0, keepdims=True))
    lb = jnp.sum(e[1:layer + 1], axis=0, keepdims=True) / jnp.sum(e, axis=0, keepdims=True)

    x = x_ref[0]
    h = _rms_norm(x, ng_ref[...]).astype(BF16)
    proj = _dot(h, w_in_ref[...])
    q = _silu(proj[:, :d])
    sig_f = _sigmoid(proj[:, d:2 * d])
    v = proj[:, 2 * d:3 * d]
    gate = proj[:, 3 * d:]
    log_f = jnp.log(lb + (1.0 - lb) * sig_f)
    k = (1.0 - lb) * (1.0 - sig_f)
    _chunked_linear_attention(q, k, v, log_f, state_ref, att_ref,
                              heads=heads, dk=dk, dv=dk, chunk=chunk)
    gated = []
    for hd in range(heads):
        vs = slice(hd * dk, (hd + 1) * dk)
        gated.append((_rms_norm(att_ref[:, vs], hn_ref[...]) * _silu(gate[:, vs])).astype(BF16))
    o_ref[0] = x + _dot(jnp.concatenate(gated, axis=1), w_out_ref[...])


def _ffn_kernel(x_ref, ng_ref, w_up_ref, w_dw_ref, w_down_ref, fg_ref, o_ref,
                tail_ref, act_ref, *, final_norm):
    rows = x_ref.shape[1]
    n_blocks, _, two_hb = w_up_ref.shape
    hb = two_hb // 2
    sub = V7X_SUBLANES

    @pl.when(pl.program_id(1) == 0)
    def _():
        tail_ref[...] = jnp.zeros_like(tail_ref)

    x = x_ref[0]
    h = _rms_norm(x, ng_ref[...]).astype(BF16)
    for j in range(n_blocks):
        z = _dot(h, w_up_ref[j])
        zc = jnp.concatenate([tail_ref[j], z], axis=0)
        tail_ref[j] = z[rows - sub:]
        w = w_dw_ref[j]
        c = (w[2:3] * z + w[1:2] * zc[sub - 1:sub - 1 + rows]
             + w[0:1] * zc[sub - 2:sub - 2 + rows])
        act_ref[:, j * hb:(j + 1) * hb] = (_silu(c[:, :hb]) * c[:, hb:]).astype(BF16)
    y = x + _dot(act_ref[...], w_down_ref[...])
    if final_norm:
        y = _rms_norm(y, fg_ref[...])
    o_ref[0] = y


def _const_spec(shape):
    nd = len(shape)
    return pl.BlockSpec(shape, lambda b, s: (0,) * nd, pipeline_mode=pl.Buffered(1))


def _layer_call(kernel_fn, x, consts, scratch_shapes, tile, name):
    bsz, seq, d = x.shape
    assert seq % tile == 0
    x_spec = pl.BlockSpec((1, tile, d), lambda b, s: (b, s, 0))
    return pl.pallas_call(
        kernel_fn,
        grid=(bsz, seq // tile),
        in_specs=[x_spec] + [_const_spec(c.shape) for c in consts],
        out_specs=x_spec,
        out_shape=jax.ShapeDtypeStruct(x.shape, x.dtype),
        scratch_shapes=scratch_shapes,
        compiler_params=pltpu.CompilerParams(
            dimension_semantics=("arbitrary", "arbitrary"),
            vmem_limit_bytes=V7X_VMEM_LIMIT_BYTES),
        name=name,
    )(x, *consts)


def _row(v):
    return v.reshape(1, -1)


def _gla_layer(x, norm_g, w_in, w_g2, b_g2, head_norm, w_out):
    d = x.shape[-1]
    heads = GLA_HEADS
    dk_t = w_g2.shape[1]
    dk = dk_t // heads
    dv = head_norm.shape[0]
    dv_t = heads * dv
    rank = w_g2.shape[0]
    n_main = 2 * dk_t + 2 * dv_t
    w_qkvr = w_in[:, :n_main].astype(BF16)
    w_glr = jnp.pad(w_in[:, n_main:], ((0, 0), (0, V7X_LANES - rank))).astype(BF16)
    w_g2p = jnp.pad(w_g2, ((0, V7X_LANES - rank), (0, 0))).astype(BF16)
    kern = functools.partial(_gla_kernel, heads=heads, dk=dk, dv=dv, chunk=GLA_CHUNK)
    consts = [_row(norm_g), w_qkvr, w_glr, w_g2p, _row(b_g2), _row(head_norm),
              w_out.astype(BF16)]
    scratch = [pltpu.VMEM((heads, dk, dv), F32), pltpu.VMEM((MIXER_TILE, dv_t), F32)]
    return _layer_call(kern, x, consts, scratch, MIXER_TILE, "gla_mixer")


def _conformer_layer(x, norm_g, w_in, b_in, w_dw, b_dw, ln_g, ln_b, w_out, b_out):
    d = x.shape[-1]
    width = w_dw.shape[0]
    halo = -(-(width - 1) // V7X_SUBLANES) * V7X_SUBLANES
    kern = functools.partial(_conformer_kernel, width=width, halo=halo)
    consts = [_row(norm_g), w_in.astype(BF16), _row(b_in), w_dw, _row(b_dw), _row(ln_g),
              _row(ln_b), w_out.astype(BF16), _row(b_out)]
    scratch = [pltpu.VMEM((halo + MIXER_TILE, d), F32)]
    return _layer_call(kern, x, consts, scratch, MIXER_TILE, "conformer_mixer")


def _sgu_layer(x, norm_g, w_in, b_in, ln_g, ln_b, w_s, b_s, w_out, b_out):
    d = x.shape[-1]
    groups, chunk, _ = w_s.shape
    b_s_full = jnp.repeat(b_s.T, d // groups, axis=1)
    kern = functools.partial(_sgu_kernel, chunk=chunk, groups=groups)
    consts = [_row(norm_g), w_in.astype(BF16), _row(b_in), _row(ln_g), _row(ln_b), w_s,
              b_s_full, w_out.astype(BF16), _row(b_out)]
    scratch = [pltpu.VMEM((MIXER_TILE, d), BF16)]
    return _layer_call(kern, x, consts, scratch, MIXER_TILE, "sgu_mixer")


def _hgrn2_layer(x, norm_g, w_in, lb_table, head_norm, w_out, layer):
    d = x.shape[-1]
    dk = head_norm.shape[0]
    heads = d // dk
    kern = functools.partial(_hgrn2_kernel, layer=layer, heads=heads, dk=dk,
                             chunk=HGRN_CHUNK)
    consts = [_row(norm_g), w_in.astype(BF16), lb_table, _row(head_norm), w_out.astype(BF16)]
    scratch = [pltpu.VMEM((heads, dk, dk), F32), pltpu.VMEM((MIXER_TILE, d), F32)]
    return _layer_call(kern, x, consts, scratch, MIXER_TILE, "hgrn2_mixer")


def _ffn_layer(x, norm_g, w_up, w_dw, w_down, final_g, final_norm):
    d = x.shape[-1]
    hidden = w_down.shape[0]
    hb = FFN_HIDDEN_BLOCK
    assert hidden % hb == 0
    nb = hidden // hb

    def blocked(w):
        r = w.shape[0]
        return jnp.concatenate([w[:, :hidden].reshape(r, nb, hb),
                                w[:, hidden:].reshape(r, nb, hb)], axis=2).transpose(1, 0, 2)

    kern = functools.partial(_ffn_kernel, final_norm=final_norm)
    consts = [_row(norm_g), blocked(w_up).astype(BF16), blocked(w_dw), w_down.astype(BF16),
              _row(final_g)]
    scratch = [pltpu.VMEM((nb, V7X_SUBLANES, 2 * hb), F32), pltpu.VMEM((FFN_TILE, hidden), BF16)]
    return _layer_call(kern, x, consts, scratch, FFN_TILE, "conv_ffn")


def kernel(x, norm_mix, norm_ffn, norm_final, gla_w_in, gla_w_g2, gla_b_g2, gla_norm, gla_w_out, cv_w_in, cv_b_in, cv_w_dw, cv_b_dw, cv_ln_g, cv_ln_b, cv_w_out, cv_b_out, sg_w_in, sg_b_in, sg_ln_g, sg_ln_b, sg_w_s, sg_b_s, sg_w_out, sg_b_out, hg_w_in, hg_lb_table, hg_norm, hg_w_out, ffn_w_up, ffn_w_dw, ffn_w_down):
    depth = norm_mix.shape[0]
    for layer in range(depth):
        m, j = layer % N_MIXERS, layer // N_MIXERS
        if m == 0:
            x = _gla_layer(x, norm_mix[layer], gla_w_in[j], gla_w_g2[j], gla_b_g2[j],
                           gla_norm[j], gla_w_out[j])
        elif m == 1:
            x = _conformer_layer(x, norm_mix[layer], cv_w_in[j], cv_b_in[j], cv_w_dw[j],
                                 cv_b_dw[j], cv_ln_g[j], cv_ln_b[j], cv_w_out[j], cv_b_out[j])
        elif m == 2:
            x = _sgu_layer(x, norm_mix[layer], sg_w_in[j], sg_b_in[j], sg_ln_g[j], sg_ln_b[j],
                           sg_w_s[j], sg_b_s[j], sg_w_out[j], sg_b_out[j])
        else:
            x = _hgrn2_layer(x, norm_mix[layer], hg_w_in[j], hg_lb_table, hg_norm[j],
                             hg_w_out[j], layer)
        x = _ffn_layer(x, norm_ffn[layer], ffn_w_up[layer], ffn_w_dw[layer],
                       ffn_w_down[layer], norm_final, final_norm=(layer == depth - 1))
    return x
```

```python
import functools

import jax
import jax.numpy as jnp
from jax import lax
from jax.experimental import pallas as pl
from jax.experimental.pallas import tpu as pltpu

F32 = jnp.float32
BF16 = jnp.bfloat16

EPS = 1e-6
N_MIXERS = 4
GLA_HEADS = 4
GLA_GATE_NORM = 16.0
GLA_CHUNK = 64
SGU_CHUNK = 128
SGU_GROUPS = 8
HGRN_EXPAND = 128
HGRN_CHUNK = 64

V7X_LANES = 128
V7X_SUBLANES = 8
V7X_VMEM_LIMIT_BYTES = 56 * 1024 * 1024

MIXER_TILE = 256
FFN_TILE = 512
FFN_HIDDEN_BLOCK = 256


def _dot(a, b):
    return jnp.dot(a.astype(BF16), b.astype(BF16), preferred_element_type=F32)


def _dot_nt(a, b):
    return lax.dot_general(a.astype(BF16), b.astype(BF16),
                           (((1,), (1,)), ((), ())), preferred_element_type=F32)


def _dot_tn(a, b):
    return lax.dot_general(a.astype(BF16), b.astype(BF16),
                           (((0,), (0,)), ((), ())), preferred_element_type=F32)


def _rms_norm(x, g):
    return x * lax.rsqrt(jnp.mean(x * x, axis=-1, keepdims=True) + EPS) * g


def _layer_norm(x, g, b):
    mu = jnp.mean(x, axis=-1, keepdims=True)
    xc = x - mu
    return xc * lax.rsqrt(jnp.mean(xc * xc, axis=-1, keepdims=True) + EPS) * g + b


def _sigmoid(x):
    return 1.0 / (1.0 + jnp.exp(-x))


def _silu(x):
    return x * _sigmoid(x)


def _log_sigmoid(x):
    return jnp.minimum(x, 0.0) - jnp.log1p(jnp.exp(-jnp.abs(x)))


def _gelu_tanh(x):
    c = 0.7978845608028654
    return 0.5 * x * (1.0 + jnp.tanh(c * (x + 0.044715 * (x * x * x))))


def _chunk_cumsum(x, chunk):
    rows = x.shape[0]
    pos = lax.broadcasted_iota(jnp.int32, (rows, 1), 0) & (chunk - 1)
    shift = 1
    while shift < chunk:
        x = x + jnp.where(pos >= shift, pltpu.roll(x, shift, axis=0), 0.0)
        shift *= 2
    return x


def _row_to_col(row):
    n = row.shape[1]
    eye = (lax.broadcasted_iota(jnp.int32, (n, n), 0)
           == lax.broadcasted_iota(jnp.int32, (n, n), 1))
    return jnp.sum(jnp.where(eye, jnp.broadcast_to(row, (n, n)), 0.0),
                   axis=1, keepdims=True)


def _chunked_linear_attention(q, k, v, log_g, state_ref, o_ref, *, heads, dk, dv, chunk):
    rows, width = q.shape
    pair = 2 * chunk
    n_pairs = rows // pair
    b = _chunk_cumsum(log_g, chunk)
    b3 = b.reshape(rows // chunk, chunk, width)
    b_mid = b3[:, chunk // 2:chunk // 2 + 1, :]
    b_last = b3[:, chunk - 1:chunk, :]
    q3 = q.reshape(b3.shape)
    k3 = k.reshape(b3.shape)
    q_in = (q3 * jnp.exp(b3 - b_mid)).astype(BF16)
    k_in = (k3 * jnp.exp(b_mid - b3)).astype(BF16)
    q_dec = q3 * jnp.exp(b3)
    k_dec = k3 * jnp.exp(b_last - b3)
    g_chunk = jnp.exp(b_last)
    causal = (lax.broadcasted_iota(jnp.int32, (pair, pair), 0)
              >= lax.broadcasted_iota(jnp.int32, (pair, pair), 1))
    zeros = jnp.zeros((chunk, dk), BF16)
    states = [state_ref[h] for h in range(heads)]
    for p in range(n_pairs):
        ca, cb = 2 * p, 2 * p + 1
        r0 = p * pair
        g_a, g_b = g_chunk[ca], g_chunk[cb]
        q_dec_b = (q_dec[cb] * g_a).astype(BF16)
        q_dec_a = q_dec[ca].astype(BF16)
        k_dec_a = k_dec[ca].astype(BF16)
        k_end_a = (k_dec[ca] * g_b).astype(BF16)
        k_end_b = k_dec[cb].astype(BF16)
        q_x_b = q_dec[cb].astype(BF16)
        g_pair_col = [None] * heads
        for h in range(heads):
            ks = slice(h * dk, (h + 1) * dk)
            vs = slice(h * dv, (h + 1) * dv)
            v_ph = v[r0:r0 + pair, vs].astype(BF16)
            q_cat = jnp.concatenate([
                jnp.concatenate([q_in[ca][:, ks], zeros, zeros], axis=1),
                jnp.concatenate([zeros, q_in[cb][:, ks], q_x_b[:, ks]], axis=1)], axis=0)
            k_cat = jnp.concatenate([
                jnp.concatenate([k_in[ca][:, ks], zeros, k_dec_a[:, ks]], axis=1),
                jnp.concatenate([zeros, k_in[cb][:, ks], zeros], axis=1)], axis=0)
            scores = jnp.where(causal, _dot_nt(q_cat, k_cat), 0.0).astype(BF16)
            q_state = jnp.concatenate([q_dec_a[:, ks], q_dec_b[:, ks]], axis=0)
            state = states[h]
            o_ref[r0:r0 + pair, vs] = _dot(
                jnp.concatenate([scores, q_state], axis=1),
                jnp.concatenate([v_ph, state.astype(BF16)], axis=0))
            k_end = jnp.concatenate([k_end_a[:, ks], k_end_b[:, ks]], axis=0)
            states[h] = (_row_to_col(g_a[:, ks] * g_b[:, ks]) * state + _dot_tn(k_end, v_ph))
    for h in range(heads):
        state_ref[h] = states[h]


def _gla_kernel(x_ref, ng_ref, w_qkvr_ref, w_glr_ref, w_g2_ref, b_g2_ref, hn_ref,
                w_out_ref, o_ref, state_ref, att_ref, *, heads, dk, dv, chunk):
    @pl.when(pl.program_id(1) == 0)
    def _():
        state_ref[...] = jnp.zeros_like(state_ref)

    x = x_ref[0]
    h = _rms_norm(x, ng_ref[...]).astype(BF16)
    dk_t, dv_t = heads * dk, heads * dv
    proj = _dot(h, w_qkvr_ref[...])
    q = proj[:, :dk_t] * (dk ** -0.5)
    k = proj[:, dk_t:2 * dk_t]
    v = proj[:, 2 * dk_t:2 * dk_t + dv_t]
    r = proj[:, 2 * dk_t + dv_t:]
    g_lr = _dot(h, w_glr_ref[...])
    log_g = _log_sigmoid(_dot(g_lr, w_g2_ref[...]) + b_g2_ref[...]) / GLA_GATE_NORM
    _chunked_linear_attention(q, k, v, log_g, state_ref, att_ref,
                              heads=heads, dk=dk, dv=dv, chunk=chunk)
    gated = []
    for hd in range(heads):
        vs = slice(hd * dv, (hd + 1) * dv)
        gated.append((_rms_norm(att_ref[:, vs], hn_ref[...]) * _silu(r[:, vs])).astype(BF16))
    o_ref[0] = x + _dot(jnp.concatenate(gated, axis=1), w_out_ref[...])


def _conformer_kernel(x_ref, ng_ref, w_in_ref, b_in_ref, w_dw_ref, b_dw_ref, ln_g_ref,
                      ln_b_ref, w_out_ref, b_out_ref, o_ref, y_ref, conv_ref, *, width, halo):
    rows = x_ref.shape[1]
    d = x_ref.shape[2]
    lanes, sub = V7X_LANES, V7X_SUBLANES
    n_slabs = d // lanes

    @pl.when(pl.program_id(1) == 0)
    def _():
        y_ref[:, 0:halo, :] = jnp.zeros((n_slabs, halo, lanes), F32)

    @pl.when(pl.program_id(1) != 0)
    def _():
        y_ref[:, 0:halo, :] = y_ref[:, rows:rows + halo, :]

    x = x_ref[0]
    h = _rms_norm(x, ng_ref[...]).astype(BF16)
    ag = _dot(h, w_in_ref[...]) + b_in_ref[...]
    glu = ag[:, :d] * _sigmoid(ag[:, d:])
    for c in range(n_slabs):
        y_ref[c, halo:halo + rows, :] = glu[:, c * lanes:(c + 1) * lanes]

    def conv_slab(c, carry):
        shifted = [y_ref[c, r:r + rows + halo - (sub if r else 0), :] for r in range(sub)]
        acc = jnp.broadcast_to(b_dw_ref[c], (rows, lanes))
        for tap in range(width):
            start = halo - (width - 1) + tap
            r, base = start % sub, start - start % sub
            acc = acc + w_dw_ref[c, tap:tap + 1, :] * shifted[r][base:base + rows]
        conv_ref[c] = acc
        return carry

    lax.fori_loop(0, n_slabs, conv_slab, 0)
    conv = jnp.concatenate([conv_ref[c] for c in range(n_slabs)], axis=1)
    z = _silu(_layer_norm(conv, ln_g_ref[...], ln_b_ref[...]))
    o_ref[0] = x + _dot(z, w_out_ref[...]) + b_out_ref[...]


def _sgu_kernel(x_ref, ng_ref, w_in_ref, b_in_ref, ln_g_ref, ln_b_ref, w_s_ref, b_s_ref,
                w_out_ref, b_out_ref, o_ref, gated_ref, *, chunk, groups):
    rows = x_ref.shape[1]
    d = x_ref.shape[2]
    gw = d // groups
    x = x_ref[0]
    h = _rms_norm(x, ng_ref[...]).astype(BF16)
    uv = _gelu_tanh(_dot(h, w_in_ref[...]) + b_in_ref[...])
    u = uv[:, :d]
    v = _layer_norm(uv[:, d:], ln_g_ref[...], ln_b_ref[...]).astype(BF16)
    causal = (lax.broadcasted_iota(jnp.int32, (chunk, chunk), 0)
              >= lax.broadcasted_iota(jnp.int32, (chunk, chunk), 1))
    for g in range(groups):
        cs = slice(g * gw, (g + 1) * gw)
        w_causal = jnp.where(causal, w_s_ref[g], 0.0).astype(BF16)
        for n in range(rows // chunk):
            rs = slice(n * chunk, (n + 1) * chunk)
            s = _dot(w_causal, v[rs, cs]) + b_s_ref[:, cs]
            gated_ref[rs, cs] = (u[rs, cs] * s).astype(BF16)
    o_ref[0] = x + _dot(gated_ref[...], w_out_ref[...]) + b_out_ref[...]


def _hgrn2_kernel(x_ref, ng_ref, w_in_ref, lb_table_ref, hn_ref, w_out_ref, o_ref,
                  state_ref, att_ref, *, layer, heads, dk, chunk):
    @pl.when(pl.program_id(1) == 0)
    def _():
        state_ref[...] = jnp.zeros_like(state_ref)

    d = x_ref.shape[2]
    table = lb_table_ref[...]
    e = jnp.exp(table - jnp.max(table, axis=0, keepdims=True))
    lb = jnp.sum(e[1:layer + 1], axis=0, keepdims=True) / jnp.sum(e, axis=0, keepdims=True)

    x = x_ref[0]
    h = _rms_norm(x, ng_ref[...]).astype(BF16)
    proj = _dot(h, w_in_ref[...])
    q = _silu(proj[:, :d])
    sig_f = _sigmoid(proj[:, d:2 * d])
    v = proj[:, 2 * d:3 * d]
    gate = proj[:, 3 * d:]
    log_f = jnp.log(lb + (1.0 - lb) * sig_f)
    k = (1.0 - lb) * (1.0 - sig_f)
    _chunked_linear_attention(q, k, v, log_f, state_ref, att_ref,
                              heads=heads, dk=dk, dv=dk, chunk=chunk)
    gated = []
    for hd in range(heads):
        vs = slice(hd * dk, (hd + 1) * dk)
        gated.append((_rms_norm(att_ref[:, vs], hn_ref[...]) * _silu(gate[:, vs])).astype(BF16))
    o_ref[0] = x + _dot(jnp.concatenate(gated, axis=1), w_out_ref[...])


def _ffn_kernel(x_ref, ng_ref, w_up_ref, w_dw_ref, w_down_ref, fg_ref, o_ref,
                tail_ref, act_ref, *, final_norm):
    rows = x_ref.shape[1]
    n_blocks, _, two_hb = w_up_ref.shape
    hb = two_hb // 2
    sub = V7X_SUBLANES

    @pl.when(pl.program_id(1) == 0)
    def _():
        tail_ref[...] = jnp.zeros_like(tail_ref)

    x = x_ref[0]
    h = _rms_norm(x, ng_ref[...]).astype(BF16)
    for j in range(n_blocks):
        z = _dot(h, w_up_ref[j])
        zc = jnp.concatenate([tail_ref[j], z], axis=0)
        tail_ref[j] = z[rows - sub:]
        w = w_dw_ref[j]
        c = (w[2:3] * z + w[1:2] * zc[sub - 1:sub - 1 + rows]
             + w[0:1] * zc[sub - 2:sub - 2 + rows])
        act_ref[:, j * hb:(j + 1) * hb] = (_silu(c[:, :hb]) * c[:, hb:]).astype(BF16)
    y = x + _dot(act_ref[...], w_down_ref[...])
    if final_norm:
        y = _rms_norm(y, fg_ref[...])
    o_ref[0] = y


def _const_spec(shape):
    nd = len(shape)
    return pl.BlockSpec(shape, lambda b, s: (0,) * nd, pipeline_mode=pl.Buffered(1))


def _layer_call(kernel_fn, x, consts, scratch_shapes, tile, name):
    bsz, seq, d = x.shape
    assert seq % tile == 0
    x_spec = pl.BlockSpec((1, tile, d), lambda b, s: (b, s, 0))
    return pl.pallas_call(
        kernel_fn,
        grid=(bsz, seq // tile),
        in_specs=[x_spec] + [_const_spec(c.shape) for c in consts],
        out_specs=x_spec,
        out_shape=jax.ShapeDtypeStruct(x.shape, x.dtype),
        scratch_shapes=scratch_shapes,
        compiler_params=pltpu.CompilerParams(
            dimension_semantics=("arbitrary", "arbitrary"),
            vmem_limit_bytes=V7X_VMEM_LIMIT_BYTES),
        name=name,
    )(x, *consts)


def _row(v):
    return v.reshape(1, -1)


def _gla_layer(x, norm_g, w_in, w_g2, b_g2, head_norm, w_out):
    d = x.shape[-1]
    heads = GLA_HEADS
    dk_t = w_g2.shape[1]
    dk = dk_t // heads
    dv = head_norm.shape[0]
    dv_t = heads * dv
    rank = w_g2.shape[0]
    n_main = 2 * dk_t + 2 * dv_t
    w_qkvr = w_in[:, :n_main].astype(BF16)
    w_glr = jnp.pad(w_in[:, n_main:], ((0, 0), (0, V7X_LANES - rank))).astype(BF16)
    w_g2p = jnp.pad(w_g2, ((0, V7X_LANES - rank), (0, 0))).astype(BF16)
    kern = functools.partial(_gla_kernel, heads=heads, dk=dk, dv=dv, chunk=GLA_CHUNK)
    consts = [_row(norm_g), w_qkvr, w_glr, w_g2p, _row(b_g2), _row(head_norm),
              w_out.astype(BF16)]
    scratch = [pltpu.VMEM((heads, dk, dv), F32), pltpu.VMEM((MIXER_TILE, dv_t), F32)]
    return _layer_call(kern, x, consts, scratch, MIXER_TILE, "gla_mixer")


def _conformer_layer(x, norm_g, w_in, b_in, w_dw, b_dw, ln_g, ln_b, w_out, b_out):
    d = x.shape[-1]
    width = w_dw.shape[0]
    halo = -(-(width - 1) // V7X_SUBLANES) * V7X_SUBLANES
    kern = functools.partial(_conformer_kernel, width=width, halo=halo)
    n_slabs = d // V7X_LANES
    w_dw_slabs = w_dw.reshape(width, n_slabs, V7X_LANES).transpose(1, 0, 2)
    b_dw_slabs = b_dw.reshape(n_slabs, 1, V7X_LANES)
    consts = [_row(norm_g), w_in.astype(BF16), _row(b_in), w_dw_slabs, b_dw_slabs, _row(ln_g),
              _row(ln_b), w_out.astype(BF16), _row(b_out)]
    scratch = [pltpu.VMEM((n_slabs, halo + MIXER_TILE, V7X_LANES), F32),
               pltpu.VMEM((n_slabs, MIXER_TILE, V7X_LANES), F32)]
    return _layer_call(kern, x, consts, scratch, MIXER_TILE, "conformer_mixer")


def _sgu_layer(x, norm_g, w_in, b_in, ln_g, ln_b, w_s, b_s, w_out, b_out):
    d = x.shape[-1]
    groups, chunk, _ = w_s.shape
    b_s_full = jnp.repeat(b_s.T, d // groups, axis=1)
    kern = functools.partial(_sgu_kernel, chunk=chunk, groups=groups)
    consts = [_row(norm_g), w_in.astype(BF16), _row(b_in), _row(ln_g), _row(ln_b), w_s,
              b_s_full, w_out.astype(BF16), _row(b_out)]
    scratch = [pltpu.VMEM((MIXER_TILE, d), BF16)]
    return _layer_call(kern, x, consts, scratch, MIXER_TILE, "sgu_mixer")


def _hgrn2_layer(x, norm_g, w_in, lb_table, head_norm, w_out, layer):
    d = x.shape[-1]
    dk = head_norm.shape[0]
    heads = d // dk
    kern = functools.partial(_hgrn2_kernel, layer=layer, heads=heads, dk=dk,
                             chunk=HGRN_CHUNK)
    consts = [_row(norm_g), w_in.astype(BF16), lb_table, _row(head_norm), w_out.astype(BF16)]
    scratch = [pltpu.VMEM((heads, dk, dk), F32), pltpu.VMEM((MIXER_TILE, d), F32)]
    return _layer_call(kern, x, consts, scratch, MIXER_TILE, "hgrn2_mixer")


def _ffn_layer(x, norm_g, w_up, w_dw, w_down, final_g, final_norm):
    d = x.shape[-1]
    hidden = w_down.shape[0]
    hb = FFN_HIDDEN_BLOCK
    assert hidden % hb == 0
    nb = hidden // hb

    def blocked(w):
        r = w.shape[0]
        return jnp.concatenate([w[:, :hidden].reshape(r, nb, hb),
                                w[:, hidden:].reshape(r, nb, hb)], axis=2).transpose(1, 0, 2)

    kern = functools.partial(_ffn_kernel, final_norm=final_norm)
    consts = [_row(norm_g), blocked(w_up).astype(BF16), blocked(w_dw), w_down.astype(BF16),
              _row(final_g)]
    scratch = [pltpu.VMEM((nb, V7X_SUBLANES, 2 * hb), F32), pltpu.VMEM((FFN_TILE, hidden), BF16)]
    return _layer_call(kern, x, consts, scratch, FFN_TILE, "conv_ffn")


def kernel(x, norm_mix, norm_ffn, norm_final, gla_w_in, gla_w_g2, gla_b_g2, gla_norm, gla_w_out, cv_w_in, cv_b_in, cv_w_dw, cv_b_dw, cv_ln_g, cv_ln_b, cv_w_out, cv_b_out, sg_w_in, sg_b_in, sg_ln_g, sg_ln_b, sg_w_s, sg_b_s, sg_w_out, sg_b_out, hg_w_in, hg_lb_table, hg_norm, hg_w_out, ffn_w_up, ffn_w_dw, ffn_w_down):
    depth = norm_mix.shape[0]
    for layer in range(depth):
        m, j = layer % N_MIXERS, layer // N_MIXERS
        if m == 0:
            x = _gla_layer(x, norm_mix[layer], gla_w_in[j], gla_w_g2[j], gla_b_g2[j],
                           gla_norm[j], gla_w_out[j])
        elif m == 1:
            x = _conformer_layer(x, norm_mix[layer], cv_w_in[j], cv_b_in[j], cv_w_dw[j],
                                 cv_b_dw[j], cv_ln_g[j], cv_ln_b[j], cv_w_out[j], cv_b_out[j])
        elif m == 2:
            x = _sgu_layer(x, norm_mix[layer], sg_w_in[j], sg_b_in[j], sg_ln_g[j], sg_ln_b[j],
                           sg_w_s[j], sg_b_s[j], sg_w_out[j], sg_b_out[j])
        else:
            x = _hgrn2_layer(x, norm_mix[layer], hg_w_in[j], hg_lb_table, hg_norm[j],
                             hg_w_out[j], layer)
        x = _ffn_layer(x, norm_ffn[layer], ffn_w_up[layer], ffn_w_dw[layer],
                       ffn_w_down[layer], norm_final, final_norm=(layer == depth - 1))
    return x
```

```python
import functools

import jax
import jax.numpy as jnp
from jax import lax
from jax.experimental import pallas as pl
from jax.experimental.pallas import tpu as pltpu

F32 = jnp.float32
BF16 = jnp.bfloat16

EPS = 1e-6
N_MIXERS = 4
GLA_HEADS = 4
GLA_GATE_NORM = 16.0
GLA_CHUNK = 64
HGRN_CHUNK = 64

V7X_LANES = 128
V7X_SUBLANES = 8
V7X_MXU_WIDTH = 256
V7X_VMEM_LIMIT_BYTES = 56 * 1024 * 1024

ATTENTION_TILE = 512
CONFORMER_TILE = 1024
SGU_TILE = 1024
FFN_TILE = 1024
FFN_HIDDEN_BLOCK = 256


def _dot(a, b):
    return jnp.dot(a.astype(BF16), b.astype(BF16), preferred_element_type=F32)


def _dot_nt(a, b):
    return lax.dot_general(a.astype(BF16), b.astype(BF16),
                           (((1,), (1,)), ((), ())), preferred_element_type=F32)


def _dot_tn(a, b):
    return lax.dot_general(a.astype(BF16), b.astype(BF16),
                           (((0,), (0,)), ((), ())), preferred_element_type=F32)


def _rms_norm(x, g):
    return x * lax.rsqrt(jnp.mean(x * x, axis=-1, keepdims=True) + EPS) * g


def _layer_norm(x, g, b):
    mu = jnp.mean(x, axis=-1, keepdims=True)
    xc = x - mu
    return xc * lax.rsqrt(jnp.mean(xc * xc, axis=-1, keepdims=True) + EPS) * g + b


def _sigmoid(x):
    return 1.0 / (1.0 + jnp.exp(-x))


def _silu(x):
    return x * _sigmoid(x)


def _log_sigmoid(x):
    return jnp.minimum(x, 0.0) - jnp.log1p(jnp.exp(-jnp.abs(x)))


def _gelu_tanh(x):
    c = 0.7978845608028654
    return 0.5 * x * (1.0 + jnp.tanh(c * (x + 0.044715 * (x * x * x))))


def _chunk_cumsum(x, chunk):
    rows = x.shape[0]
    pos = lax.broadcasted_iota(jnp.int32, (rows, 1), 0) & (chunk - 1)
    shift = 1
    while shift < chunk:
        x = x + jnp.where(pos >= shift, pltpu.roll(x, shift, axis=0), 0.0)
        shift *= 2
    return x


def _row_to_col(row):
    n = row.shape[1]
    eye = (lax.broadcasted_iota(jnp.int32, (n, n), 0)
           == lax.broadcasted_iota(jnp.int32, (n, n), 1))
    return jnp.sum(jnp.where(eye, jnp.broadcast_to(row, (n, n)), 0.0),
                   axis=1, keepdims=True)


def _chunked_linear_attention(q, k, load_v, log_g, state_ref, o_ref, *, heads, dk, dv, chunk,
                              pair_chunks):
    rows, width = q.shape
    n_chunks = rows // chunk
    b = _chunk_cumsum(log_g, chunk)
    b3 = b.reshape(n_chunks, chunk, width)
    b_mid = b3[:, chunk // 2:chunk // 2 + 1, :]
    b_last = b3[:, chunk - 1:chunk, :]
    q3 = q.reshape(b3.shape)
    k3 = k.reshape(b3.shape)
    q_in = (q3 * jnp.exp(b3 - b_mid)).astype(BF16)
    k_in = (k3 * jnp.exp(b_mid - b3)).astype(BF16)
    q_dec = q3 * jnp.exp(b3)
    k_dec = k3 * jnp.exp(b_last - b3)
    g_chunk = jnp.exp(b_last)
    states = [state_ref[h] for h in range(heads)]
    if not pair_chunks:
        causal = (lax.broadcasted_iota(jnp.int32, (chunk, chunk), 0)
                  >= lax.broadcasted_iota(jnp.int32, (chunk, chunk), 1))
        for n in range(n_chunks):
            rs = slice(n * chunk, (n + 1) * chunk)
            q_dec_n = q_dec[n].astype(BF16)
            k_dec_n = k_dec[n].astype(BF16)
            for h in range(heads):
                ks = slice(h * dk, (h + 1) * dk)
                vs = slice(h * dv, (h + 1) * dv)
                v_nh = load_v(rs, vs).astype(BF16)
                scores = jnp.where(causal, _dot_nt(q_in[n][:, ks], k_in[n][:, ks]), 0.0)
                o_ref[rs, vs] = _dot(scores, v_nh) + _dot(q_dec_n[:, ks], states[h])
                states[h] = (_row_to_col(g_chunk[n][:, ks]) * states[h]
                             + _dot_tn(k_dec_n[:, ks], v_nh))
    else:
        pair = 2 * chunk
        causal = (lax.broadcasted_iota(jnp.int32, (pair, pair), 0)
                  >= lax.broadcasted_iota(jnp.int32, (pair, pair), 1))
        zeros = jnp.zeros((chunk, dk), BF16)
        for p in range(n_chunks // 2):
            ca, cb = 2 * p, 2 * p + 1
            rs = slice(p * pair, (p + 1) * pair)
            g_a, g_b = g_chunk[ca], g_chunk[cb]
            q_dec_b = (q_dec[cb] * g_a).astype(BF16)
            q_dec_a = q_dec[ca].astype(BF16)
            k_dec_a = k_dec[ca].astype(BF16)
            k_end_a = (k_dec[ca] * g_b).astype(BF16)
            k_end_b = k_dec[cb].astype(BF16)
            q_x_b = q_dec[cb].astype(BF16)
            for h in range(heads):
                ks = slice(h * dk, (h + 1) * dk)
                vs = slice(h * dv, (h + 1) * dv)
                v_ph = load_v(rs, vs).astype(BF16)
                q_cat = jnp.concatenate([
                    jnp.concatenate([q_in[ca][:, ks], zeros, zeros], axis=1),
                    jnp.concatenate([zeros, q_in[cb][:, ks], q_x_b[:, ks]], axis=1)], axis=0)
                k_cat = jnp.concatenate([
                    jnp.concatenate([k_in[ca][:, ks], zeros, k_dec_a[:, ks]], axis=1),
                    jnp.concatenate([zeros, k_in[cb][:, ks], zeros], axis=1)], axis=0)
                scores = jnp.where(causal, _dot_nt(q_cat, k_cat), 0.0).astype(BF16)
                q_state = jnp.concatenate([q_dec_a[:, ks], q_dec_b[:, ks]], axis=0)
                o_ref[rs, vs] = _dot(
                    jnp.concatenate([scores, q_state], axis=1),
                    jnp.concatenate([v_ph, states[h].astype(BF16)], axis=0))
                k_end = jnp.concatenate([k_end_a[:, ks], k_end_b[:, ks]], axis=0)
                states[h] = (_row_to_col(g_a[:, ks] * g_b[:, ks]) * states[h]
                             + _dot_tn(k_end, v_ph))
    for h in range(heads):
        state_ref[h] = states[h]


def _gla_kernel(x_ref, ng_ref, w_in_ref, w_g2_ref, b_g2_ref, hn_ref, w_out_ref, o_ref,
                state_ref, att_ref, vr_ref, *, heads, dk, dv, chunk):
    @pl.when(pl.program_id(1) == 0)
    def _():
        state_ref[...] = jnp.zeros_like(state_ref)

    dk_t, dv_t = heads * dk, heads * dv
    n_main = 2 * dk_t + 2 * dv_t
    x = x_ref[0]
    h = _rms_norm(x, ng_ref[...]).astype(BF16)
    g_lr = _dot(h, w_in_ref[:, n_main:])
    qk = _dot(h, w_in_ref[:, :2 * dk_t])
    gate = _dot(g_lr, w_g2_ref[...])
    vr_ref[...] = _dot(h, w_in_ref[:, 2 * dk_t:n_main])
    log_g = _log_sigmoid(gate + b_g2_ref[...]) / GLA_GATE_NORM
    _chunked_linear_attention(qk[:, :dk_t] * (dk ** -0.5), qk[:, dk_t:],
                              lambda rs, cs: vr_ref[rs, cs], log_g, state_ref, att_ref,
                              heads=heads, dk=dk, dv=dv, chunk=chunk, pair_chunks=True)
    gated = []
    for hd in range(heads):
        vs = slice(hd * dv, (hd + 1) * dv)
        r = vr_ref[:, dv_t + hd * dv:dv_t + (hd + 1) * dv]
        gated.append((_rms_norm(att_ref[:, vs], hn_ref[...]) * _silu(r)).astype(BF16))
    o_ref[0] = x + _dot(jnp.concatenate(gated, axis=1), w_out_ref[...])


def _conformer_kernel(x_ref, ng_ref, w_in_ref, b_in_ref, w_dw_ref, b_dw_ref, ln_g_ref,
                      ln_b_ref, w_out_ref, b_out_ref, o_ref, y_ref, conv_ref, *, width, halo):
    rows = x_ref.shape[1]
    d = x_ref.shape[2]
    lanes, sub = V7X_LANES, V7X_SUBLANES
    n_slabs = d // lanes

    @pl.when(pl.program_id(1) == 0)
    def _():
        y_ref[:, 0:halo, :] = jnp.zeros((n_slabs, halo, lanes), F32)

    @pl.when(pl.program_id(1) != 0)
    def _():
        y_ref[:, 0:halo, :] = y_ref[:, rows:rows + halo, :]

    x = x_ref[0]
    h = _rms_norm(x, ng_ref[...]).astype(BF16)
    ag = _dot(h, w_in_ref[...]) + b_in_ref[...]
    glu = ag[:, :d] * _sigmoid(ag[:, d:])
    for c in range(n_slabs):
        y_ref[c, halo:halo + rows, :] = glu[:, c * lanes:(c + 1) * lanes]

    def conv_slab(c, carry):
        shifted = [y_ref[c, r:r + rows + halo - (sub if r else 0), :] for r in range(sub)]
        acc = jnp.broadcast_to(b_dw_ref[c], (rows, lanes))
        for tap in range(width):
            start = halo - (width - 1) + tap
            r, base = start % sub, start - start % sub
            acc = acc + w_dw_ref[c, tap:tap + 1, :] * shifted[r][base:base + rows]
        conv_ref[c] = acc
        return carry

    lax.fori_loop(0, n_slabs, conv_slab, 0)
    conv = jnp.concatenate([conv_ref[c] for c in range(n_slabs)], axis=1)
    z = _silu(_layer_norm(conv, ln_g_ref[...], ln_b_ref[...]))
    o_ref[0] = x + _dot(z, w_out_ref[...]) + b_out_ref[...]


def _sgu_kernel(x_ref, ng_ref, w_in_ref, b_in_ref, ln_g_ref, ln_b_ref, w_s_ref, b_s_ref,
                w_out_ref, b_out_ref, o_ref, gated_ref, u_ref, *, chunk, groups):
    rows = x_ref.shape[1]
    d = x_ref.shape[2]
    gw = d // groups
    x = x_ref[0]
    h = _rms_norm(x, ng_ref[...]).astype(BF16)
    v = _dot(h, w_in_ref[:, d:])
    u_ref[...] = _dot(h, w_in_ref[:, :d])
    v = _gelu_tanh(v + b_in_ref[:, d:])
    v = _layer_norm(v, ln_g_ref[...], ln_b_ref[...]).astype(BF16)
    causal = (lax.broadcasted_iota(jnp.int32, (chunk, chunk), 0)
              >= lax.broadcasted_iota(jnp.int32, (chunk, chunk), 1))
    for g in range(groups):
        cs = slice(g * gw, (g + 1) * gw)
        w_causal = jnp.where(causal, w_s_ref[g], 0.0).astype(BF16)
        u = _gelu_tanh(u_ref[:, cs] + b_in_ref[:, cs])
        for n in range(rows // chunk):
            rs = slice(n * chunk, (n + 1) * chunk)
            s = _dot(w_causal, v[rs, cs]) + b_s_ref[:, cs]
            gated_ref[rs, cs] = (u[rs] * s).astype(BF16)
    o_ref[0] = x + _dot(gated_ref[...], w_out_ref[...]) + b_out_ref[...]


def _hgrn2_kernel(x_ref, ng_ref, w_in_ref, lb_table_ref, hn_ref, w_out_ref, o_ref,
                  state_ref, att_ref, vg_ref, *, layer, heads, dk, chunk):
    @pl.when(pl.program_id(1) == 0)
    def _():
        state_ref[...] = jnp.zeros_like(state_ref)

    d = x_ref.shape[2]
    table = lb_table_ref[...]
    e = jnp.exp(table - jnp.max(table, axis=0, keepdims=True))
    lb = jnp.sum(e[1:layer + 1], axis=0, keepdims=True) / jnp.sum(e, axis=0, keepdims=True)

    x = x_ref[0]
    h = _rms_norm(x, ng_ref[...]).astype(BF16)
    qf = _dot(h, w_in_ref[:, :2 * d])
    q = _silu(qf[:, :d])
    sig_f = _sigmoid(qf[:, d:])
    log_f = jnp.log(lb + (1.0 - lb) * sig_f)
    k = (1.0 - lb) * (1.0 - sig_f)
    vg_ref[...] = _dot(h, w_in_ref[:, 2 * d:])
    _chunked_linear_attention(q, k, lambda rs, cs: vg_ref[rs, cs], log_f, state_ref, att_ref,
                              heads=heads, dk=dk, dv=dk, chunk=chunk, pair_chunks=False)
    gated = []
    for hd in range(heads):
        vs = slice(hd * dk, (hd + 1) * dk)
        gate = vg_ref[:, d + hd * dk:d + (hd + 1) * dk]
        gated.append((_rms_norm(att_ref[:, vs], hn_ref[...]) * _silu(gate)).astype(BF16))
    o_ref[0] = x + _dot(jnp.concatenate(gated, axis=1), w_out_ref[...])


def _ffn_kernel(x_ref, ng_ref, w_up_ref, w_dw_ref, w_down_ref, fg_ref, o_ref,
                tail_ref, act_ref, *, final_norm):
    rows = x_ref.shape[1]
    n_blocks, _, two_hb = w_up_ref.shape
    hb = two_hb // 2
    sub = V7X_SUBLANES

    @pl.when(pl.program_id(1) == 0)
    def _():
        tail_ref[...] = jnp.zeros_like(tail_ref)

    x = x_ref[0]
    h = _rms_norm(x, ng_ref[...]).astype(BF16)
    for j in range(n_blocks):
        z = _dot(h, w_up_ref[j])
        zc = jnp.concatenate([tail_ref[j], z], axis=0)
        tail_ref[j] = z[rows - sub:]
        w = w_dw_ref[j]
        c = (w[2:3] * z + w[1:2] * zc[sub - 1:sub - 1 + rows]
             + w[0:1] * zc[sub - 2:sub - 2 + rows])
        act_ref[:, j * hb:(j + 1) * hb] = (_silu(c[:, :hb]) * c[:, hb:]).astype(BF16)
    y = x + _dot(act_ref[...], w_down_ref[...])
    if final_norm:
        y = _rms_norm(y, fg_ref[...])
    o_ref[0] = y


def _layer_call(kernel_fn, x, consts, scratch_shapes, tile, name):
    bsz, seq, d = x.shape
    assert seq % tile == 0
    x_spec = pl.BlockSpec((1, tile, d), lambda b, s: (b, s, 0))

    def const_spec(shape):
        nd = len(shape)
        return pl.BlockSpec(shape, lambda b, s: (0,) * nd, pipeline_mode=pl.Buffered(1))

    return pl.pallas_call(
        kernel_fn,
        grid=(bsz, seq // tile),
        in_specs=[x_spec] + [const_spec(c.shape) for c in consts],
        out_specs=x_spec,
        out_shape=jax.ShapeDtypeStruct(x.shape, x.dtype),
        scratch_shapes=scratch_shapes,
        compiler_params=pltpu.CompilerParams(
            dimension_semantics=("arbitrary", "arbitrary"),
            vmem_limit_bytes=V7X_VMEM_LIMIT_BYTES),
        name=name,
    )(x, *consts)


def _row(v):
    return v.reshape(1, -1)


def _gla_layer(x, norm_g, w_in, w_g2, b_g2, head_norm, w_out):
    heads = GLA_HEADS
    dk_t = w_g2.shape[1]
    dk = dk_t // heads
    dv = head_norm.shape[0]
    dv_t = heads * dv
    rank = w_g2.shape[0]
    w_in_p = jnp.pad(w_in, ((0, 0), (0, V7X_LANES - rank))).astype(BF16)
    w_g2_p = jnp.pad(w_g2, ((0, V7X_LANES - rank), (0, 0))).astype(BF16)
    kern = functools.partial(_gla_kernel, heads=heads, dk=dk, dv=dv, chunk=GLA_CHUNK)
    consts = [_row(norm_g), w_in_p, w_g2_p, _row(b_g2), _row(head_norm), w_out.astype(BF16)]
    scratch = [pltpu.VMEM((heads, dk, dv), F32), pltpu.VMEM((ATTENTION_TILE, dv_t), F32),
               pltpu.VMEM((ATTENTION_TILE, 2 * dv_t), F32)]
    return _layer_call(kern, x, consts, scratch, ATTENTION_TILE, "gla_mixer")


def _conformer_layer(x, norm_g, w_in, b_in, w_dw, b_dw, ln_g, ln_b, w_out, b_out):
    d = x.shape[-1]
    width = w_dw.shape[0]
    lanes = V7X_LANES
    halo = -(-(width - 1) // V7X_SUBLANES) * V7X_SUBLANES
    n_slabs = d // lanes
    w_dw_slabs = w_dw.reshape(width, n_slabs, lanes).transpose(1, 0, 2)
    b_dw_slabs = b_dw.reshape(n_slabs, 1, lanes)
    kern = functools.partial(_conformer_kernel, width=width, halo=halo)
    consts = [_row(norm_g), w_in.astype(BF16), _row(b_in), w_dw_slabs, b_dw_slabs, _row(ln_g),
              _row(ln_b), w_out.astype(BF16), _row(b_out)]
    scratch = [pltpu.VMEM((n_slabs, halo + CONFORMER_TILE, lanes), F32),
               pltpu.VMEM((n_slabs, CONFORMER_TILE, lanes), F32)]
    return _layer_call(kern, x, consts, scratch, CONFORMER_TILE, "conformer_mixer")


def _sgu_layer(x, norm_g, w_in, b_in, ln_g, ln_b, w_s, b_s, w_out, b_out):
    d = x.shape[-1]
    groups, chunk, _ = w_s.shape
    b_s_full = jnp.repeat(b_s.T, d // groups, axis=1)
    kern = functools.partial(_sgu_kernel, chunk=chunk, groups=groups)
    consts = [_row(norm_g), w_in.astype(BF16), _row(b_in), _row(ln_g), _row(ln_b), w_s,
              b_s_full, w_out.astype(BF16), _row(b_out)]
    scratch = [pltpu.VMEM((SGU_TILE, d), BF16), pltpu.VMEM((SGU_TILE, d), F32)]
    return _layer_call(kern, x, consts, scratch, SGU_TILE, "sgu_mixer")


def _hgrn2_layer(x, norm_g, w_in, lb_table, head_norm, w_out, layer):
    d = x.shape[-1]
    dk = head_norm.shape[0]
    heads = d // dk
    kern = functools.partial(_hgrn2_kernel, layer=layer, heads=heads, dk=dk,
                             chunk=HGRN_CHUNK)
    consts = [_row(norm_g), w_in.astype(BF16), lb_table, _row(head_norm), w_out.astype(BF16)]
    scratch = [pltpu.VMEM((heads, dk, dk), F32), pltpu.VMEM((ATTENTION_TILE, d), F32),
               pltpu.VMEM((ATTENTION_TILE, 2 * d), F32)]
    return _layer_call(kern, x, consts, scratch, ATTENTION_TILE, "hgrn2_mixer")


def _ffn_layer(x, norm_g, w_up, w_dw, w_down, final_g, final_norm):
    hidden = w_down.shape[0]
    hb = FFN_HIDDEN_BLOCK
    assert hidden % hb == 0
    nb = hidden // hb

    def blocked(w):
        r = w.shape[0]
        return jnp.concatenate([w[:, :hidden].reshape(r, nb, hb),
                                w[:, hidden:].reshape(r, nb, hb)], axis=2).transpose(1, 0, 2)

    kern = functools.partial(_ffn_kernel, final_norm=final_norm)
    consts = [_row(norm_g), blocked(w_up).astype(BF16), blocked(w_dw), w_down.astype(BF16),
              _row(final_g)]
    scratch = [pltpu.VMEM((nb, V7X_SUBLANES, 2 * hb), F32), pltpu.VMEM((FFN_TILE, hidden), BF16)]
    return _layer_call(kern, x, consts, scratch, FFN_TILE, "conv_ffn")


def kernel(x, norm_mix, norm_ffn, norm_final, gla_w_in, gla_w_g2, gla_b_g2, gla_norm, gla_w_out, cv_w_in, cv_b_in, cv_w_dw, cv_b_dw, cv_ln_g, cv_ln_b, cv_w_out, cv_b_out, sg_w_in, sg_b_in, sg_ln_g, sg_ln_b, sg_w_s, sg_b_s, sg_w_out, sg_b_out, hg_w_in, hg_lb_table, hg_norm, hg_w_out, ffn_w_up, ffn_w_dw, ffn_w_down):
    depth = norm_mix.shape[0]
    for layer in range(depth):
        m, j = layer % N_MIXERS, layer // N_MIXERS
        if m == 0:
            x = _gla_layer(x, norm_mix[layer], gla_w_in[j], gla_w_g2[j], gla_b_g2[j],
                           gla_norm[j], gla_w_out[j])
        elif m == 1:
            x = _conformer_layer(x, norm_mix[layer], cv_w_in[j], cv_b_in[j], cv_w_dw[j],
                                 cv_b_dw[j], cv_ln_g[j], cv_ln_b[j], cv_w_out[j], cv_b_out[j])
        elif m == 2:
            x = _sgu_layer(x, norm_mix[layer], sg_w_in[j], sg_b_in[j], sg_ln_g[j], sg_ln_b[j],
                           sg_w_s[j], sg_b_s[j], sg_w_out[j], sg_b_out[j])
        else:
            x = _hgrn2_layer(x, norm_mix[layer], hg_w_in[j], hg_lb_table, hg_norm[j],
                             hg_w_out[j], layer)
        x = _ffn_layer(x, norm_ffn[layer], ffn_w_up[layer], ffn_w_dw[layer],
                       ffn_w_down[layer], norm_final, final_norm=(layer == depth - 1))
    return x
```

```python
import functools

import jax
import jax.numpy as jnp
from jax import lax
from jax.experimental import pallas as pl
from jax.experimental.pallas import tpu as pltpu

F32 = jnp.float32
BF16 = jnp.bfloat16

EPS = 1e-6
N_MIXERS = 4
GLA_HEADS = 4
GLA_GATE_NORM = 16.0
GLA_CHUNK = 64
HGRN_CHUNK = 64

V7X_LANES = 128
V7X_SUBLANES = 8
V7X_MXU_WIDTH = 256
V7X_VMEM_LIMIT_BYTES = 56 * 1024 * 1024

GLA_TILE = 1024
HGRN_TILE = 512
CONFORMER_TILE = 1024
SGU_TILE = 1024
FFN_TILE = 1024
PROJ_BLOCK = 2 * V7X_MXU_WIDTH
FFN_HIDDEN_BLOCK = 256


def _dot(a, b):
    return jnp.dot(a.astype(BF16), b.astype(BF16), preferred_element_type=F32)


def _dot_nt(a, b):
    return lax.dot_general(a.astype(BF16), b.astype(BF16),
                           (((1,), (1,)), ((), ())), preferred_element_type=F32)


def _dot_tn(a, b):
    return lax.dot_general(a.astype(BF16), b.astype(BF16),
                           (((0,), (0,)), ((), ())), preferred_element_type=F32)


def _rms_norm(x, g):
    return x * lax.rsqrt(jnp.mean(x * x, axis=-1, keepdims=True) + EPS) * g


def _layer_norm(x, g, b):
    mu = jnp.mean(x, axis=-1, keepdims=True)
    xc = x - mu
    return xc * lax.rsqrt(jnp.mean(xc * xc, axis=-1, keepdims=True) + EPS) * g + b


def _sigmoid(x):
    return 1.0 / (1.0 + jnp.exp(-x))


def _silu(x):
    return x * _sigmoid(x)


def _log_sigmoid(x):
    return jnp.minimum(x, 0.0) - jnp.log1p(jnp.exp(-jnp.abs(x)))


def _gelu_tanh(x):
    c = 0.7978845608028654
    return 0.5 * x * (1.0 + jnp.tanh(c * (x + 0.044715 * (x * x * x))))


def _chunk_cumsum(x, chunk):
    rows = x.shape[0]
    pos = lax.broadcasted_iota(jnp.int32, (rows, 1), 0) & (chunk - 1)
    shift = 1
    while shift < chunk:
        x = x + jnp.where(pos >= shift, pltpu.roll(x, shift, axis=0), 0.0)
        shift *= 2
    return x


def _row_to_col(row):
    n = row.shape[1]
    eye = (lax.broadcasted_iota(jnp.int32, (n, n), 0)
           == lax.broadcasted_iota(jnp.int32, (n, n), 1))
    return jnp.sum(jnp.where(eye, jnp.broadcast_to(row, (n, n)), 0.0),
                   axis=1, keepdims=True)


def _attention_operands(q, k, log_g, chunk):
    rows, width = q.shape
    b3 = _chunk_cumsum(log_g, chunk).reshape(rows // chunk, chunk, width)
    b_mid = b3[:, chunk // 2:chunk // 2 + 1, :]
    b_last = b3[:, chunk - 1:chunk, :]
    q3 = q.reshape(b3.shape)
    k3 = k.reshape(b3.shape)
    q_in = (q3 * jnp.exp(b3 - b_mid)).astype(BF16)
    k_in = (k3 * jnp.exp(b_mid - b3)).astype(BF16)
    q_dec = q3 * jnp.exp(b3)
    k_dec = k3 * jnp.exp(b_last - b3)
    return q_in, k_in, q_dec, k_dec, jnp.exp(b_last)


def _attention_apply(blocks, load_v, state_ref, o_ref, *, heads, dk, dv, chunk, pair_chunks):
    heads_per_block = heads // len(blocks)
    n_chunks = blocks[0][0].shape[0]
    states = [state_ref[h] for h in range(heads)]
    if not pair_chunks:
        causal = (lax.broadcasted_iota(jnp.int32, (chunk, chunk), 0)
                  >= lax.broadcasted_iota(jnp.int32, (chunk, chunk), 1))
        for n in range(n_chunks):
            rs = slice(n * chunk, (n + 1) * chunk)
            for h in range(heads):
                q_in, k_in, q_dec, k_dec, g_chunk = blocks[h // heads_per_block]
                ks = slice((h % heads_per_block) * dk, (h % heads_per_block + 1) * dk)
                vs = slice(h * dv, (h + 1) * dv)
                v_nh = load_v(rs, vs).astype(BF16)
                scores = jnp.where(causal, _dot_nt(q_in[n][:, ks], k_in[n][:, ks]), 0.0)
                o_ref[rs, vs] = _dot(scores, v_nh) + _dot(q_dec[n][:, ks], states[h])
                states[h] = (_row_to_col(g_chunk[n][:, ks]) * states[h]
                             + _dot_tn(k_dec[n][:, ks], v_nh))
    else:
        pair = 2 * chunk
        causal = (lax.broadcasted_iota(jnp.int32, (pair, pair), 0)
                  >= lax.broadcasted_iota(jnp.int32, (pair, pair), 1))
        zeros = jnp.zeros((chunk, dk), BF16)
        for p in range(n_chunks // 2):
            ca, cb = 2 * p, 2 * p + 1
            rs = slice(p * pair, (p + 1) * pair)
            for h in range(heads):
                q_in, k_in, q_dec, k_dec, g_chunk = blocks[h // heads_per_block]
                ks = slice((h % heads_per_block) * dk, (h % heads_per_block + 1) * dk)
                vs = slice(h * dv, (h + 1) * dv)
                g_a, g_b = g_chunk[ca][:, ks], g_chunk[cb][:, ks]
                q_dec_a, q_dec_b = q_dec[ca][:, ks], q_dec[cb][:, ks]
                k_dec_a, k_dec_b = k_dec[ca][:, ks], k_dec[cb][:, ks]
                v_ph = load_v(rs, vs).astype(BF16)
                q_cat = jnp.concatenate([
                    jnp.concatenate([q_in[ca][:, ks], zeros, zeros], axis=1),
                    jnp.concatenate([zeros, q_in[cb][:, ks], q_dec_b.astype(BF16)], axis=1)],
                    axis=0)
                k_cat = jnp.concatenate([
                    jnp.concatenate([k_in[ca][:, ks], zeros, k_dec_a.astype(BF16)], axis=1),
                    jnp.concatenate([zeros, k_in[cb][:, ks], zeros], axis=1)], axis=0)
                scores = jnp.where(causal, _dot_nt(q_cat, k_cat), 0.0).astype(BF16)
                q_state = jnp.concatenate([q_dec_a.astype(BF16), (q_dec_b * g_a).astype(BF16)],
                                          axis=0)
                o_ref[rs, vs] = _dot(
                    jnp.concatenate([scores, q_state], axis=1),
                    jnp.concatenate([v_ph, states[h].astype(BF16)], axis=0))
                k_end = jnp.concatenate([(k_dec_a * g_b).astype(BF16), k_dec_b.astype(BF16)],
                                        axis=0)
                states[h] = _row_to_col(g_a * g_b) * states[h] + _dot_tn(k_end, v_ph)
    for h in range(heads):
        state_ref[h] = states[h]


def _gla_kernel(x_ref, ng_ref, w_in_ref, w_g2_ref, b_g2_ref, hn_ref, w_out_ref, o_ref,
                state_ref, att_ref, vr_ref, *, heads, dk, dv, chunk):
    @pl.when(pl.program_id(1) == 0)
    def _():
        state_ref[...] = jnp.zeros_like(state_ref)

    dk_t, dv_t = heads * dk, heads * dv
    n_main = 2 * dk_t + 2 * dv_t
    x = x_ref[0]
    h = _rms_norm(x, ng_ref[...]).astype(BF16)
    g_lr = _dot(h, w_in_ref[:, n_main:])
    q = _dot(h, w_in_ref[:, :dk_t]) * (dk ** -0.5)
    k = _dot(h, w_in_ref[:, dk_t:2 * dk_t])
    log_g = _log_sigmoid(_dot(g_lr, w_g2_ref[...]) + b_g2_ref[...]) / GLA_GATE_NORM
    blocks = [_attention_operands(q, k, log_g, chunk)]
    vr_ref[...] = _dot(h, w_in_ref[:, 2 * dk_t:n_main])
    _attention_apply(blocks, lambda rs, cs: vr_ref[rs, cs], state_ref, att_ref,
                     heads=heads, dk=dk, dv=dv, chunk=chunk, pair_chunks=True)
    gated = []
    for hd in range(heads):
        vs = slice(hd * dv, (hd + 1) * dv)
        r = vr_ref[:, dv_t + hd * dv:dv_t + (hd + 1) * dv]
        gated.append((_rms_norm(att_ref[:, vs], hn_ref[...]) * _silu(r)).astype(BF16))
    o_ref[0] = x + _dot(jnp.concatenate(gated, axis=1), w_out_ref[...])


def _conformer_kernel(x_ref, ng_ref, w_in_ref, b_in_ref, w_dw_ref, b_dw_ref, ln_g_ref,
                      ln_b_ref, w_out_ref, b_out_ref, o_ref, y_ref, conv_ref, *, width, halo):
    rows = x_ref.shape[1]
    d = x_ref.shape[2]
    lanes, sub = V7X_LANES, V7X_SUBLANES
    n_slabs = d // lanes
    n_blocks, _, two_bw = w_in_ref.shape
    bw = two_bw // 2

    @pl.when(pl.program_id(1) == 0)
    def _():
        y_ref[:, 0:halo, :] = jnp.zeros((n_slabs, halo, lanes), F32)

    @pl.when(pl.program_id(1) != 0)
    def _():
        y_ref[:, 0:halo, :] = y_ref[:, rows:rows + halo, :]

    x = x_ref[0]
    h = _rms_norm(x, ng_ref[...]).astype(BF16)
    for j in range(n_blocks):
        ag = _dot(h, w_in_ref[j]) + b_in_ref[j]
        glu = ag[:, :bw] * _sigmoid(ag[:, bw:])
        for s in range(bw // lanes):
            y_ref[j * (bw // lanes) + s, halo:halo + rows, :] = glu[:, s * lanes:(s + 1) * lanes]

    def conv_slab(c, carry):
        shifted = [y_ref[c, r:r + rows + halo - (sub if r else 0), :] for r in range(sub)]
        acc = jnp.broadcast_to(b_dw_ref[c], (rows, lanes))
        for tap in range(width):
            start = halo - (width - 1) + tap
            r, base = start % sub, start - start % sub
            acc = acc + w_dw_ref[c, tap:tap + 1, :] * shifted[r][base:base + rows]
        conv_ref[c] = acc
        return carry

    lax.fori_loop(0, n_slabs, conv_slab, 0)
    conv = jnp.concatenate([conv_ref[c] for c in range(n_slabs)], axis=1)
    z = _silu(_layer_norm(conv, ln_g_ref[...], ln_b_ref[...]))
    o_ref[0] = x + _dot(z, w_out_ref[...]) + b_out_ref[...]


def _sgu_kernel(x_ref, ng_ref, w_in_ref, b_in_ref, ln_g_ref, ln_b_ref, w_s_ref, b_s_ref,
                w_out_ref, b_out_ref, o_ref, gated_ref, u_ref, *, chunk, groups):
    rows = x_ref.shape[1]
    d = x_ref.shape[2]
    gw = d // groups
    x = x_ref[0]
    h = _rms_norm(x, ng_ref[...]).astype(BF16)

    def gelu_block(c0):
        cs = slice(c0, c0 + PROJ_BLOCK)
        return _gelu_tanh(_dot(h, w_in_ref[:, cs]) + b_in_ref[:, cs])

    v = jnp.concatenate([gelu_block(d + c0) for c0 in range(0, d, PROJ_BLOCK)], axis=1)
    for c0 in range(0, d, PROJ_BLOCK):
        u_ref[:, c0:c0 + PROJ_BLOCK] = gelu_block(c0)
    v = _layer_norm(v, ln_g_ref[...], ln_b_ref[...]).astype(BF16)
    causal = (lax.broadcasted_iota(jnp.int32, (chunk, chunk), 0)
              >= lax.broadcasted_iota(jnp.int32, (chunk, chunk), 1))
    for g in range(groups):
        cs = slice(g * gw, (g + 1) * gw)
        w_causal = jnp.where(causal, w_s_ref[g], 0.0).astype(BF16)
        for n in range(rows // chunk):
            rs = slice(n * chunk, (n + 1) * chunk)
            s = _dot(w_causal, v[rs, cs]) + b_s_ref[:, cs]
            gated_ref[rs, cs] = (u_ref[rs, cs] * s).astype(BF16)
    o_ref[0] = x + _dot(gated_ref[...], w_out_ref[...]) + b_out_ref[...]


def _hgrn2_kernel(x_ref, ng_ref, w_in_ref, lb_table_ref, hn_ref, w_out_ref, o_ref,
                  state_ref, att_ref, vg_ref, *, layer, heads, dk, chunk):
    @pl.when(pl.program_id(1) == 0)
    def _():
        state_ref[...] = jnp.zeros_like(state_ref)

    d = x_ref.shape[2]
    table = lb_table_ref[...]
    e = jnp.exp(table - jnp.max(table, axis=0, keepdims=True))
    lb = jnp.sum(e[1:layer + 1], axis=0, keepdims=True) / jnp.sum(e, axis=0, keepdims=True)

    x = x_ref[0]
    h = _rms_norm(x, ng_ref[...]).astype(BF16)
    blocks = []
    for c0 in range(0, d, PROJ_BLOCK):
        cs = slice(c0, c0 + PROJ_BLOCK)
        q = _silu(_dot(h, w_in_ref[:, cs]))
        sig_f = _sigmoid(_dot(h, w_in_ref[:, d + c0:d + c0 + PROJ_BLOCK]))
        log_f = jnp.log(lb[:, cs] + (1.0 - lb[:, cs]) * sig_f)
        k = (1.0 - lb[:, cs]) * (1.0 - sig_f)
        blocks.append(_attention_operands(q, k, log_f, chunk))
    vg_ref[...] = _dot(h, w_in_ref[:, 2 * d:])
    _attention_apply(blocks, lambda rs, cs: vg_ref[rs, cs], state_ref, att_ref,
                     heads=heads, dk=dk, dv=dk, chunk=chunk, pair_chunks=False)
    gated = []
    for hd in range(heads):
        vs = slice(hd * dk, (hd + 1) * dk)
        gate = vg_ref[:, d + hd * dk:d + (hd + 1) * dk]
        gated.append((_rms_norm(att_ref[:, vs], hn_ref[...]) * _silu(gate)).astype(BF16))
    o_ref[0] = x + _dot(jnp.concatenate(gated, axis=1), w_out_ref[...])


def _ffn_kernel(x_ref, ng_ref, w_up_ref, w_dw_ref, w_down_ref, fg_ref, o_ref,
                tail_ref, act_ref, z_ref, *, final_norm):
    rows = x_ref.shape[1]
    n_blocks, _, two_hb = w_up_ref.shape
    hb = two_hb // 2
    sub, lanes = V7X_SUBLANES, V7X_LANES
    gate_slabs = hb // lanes

    @pl.when(pl.program_id(1) == 0)
    def _():
        tail_ref[...] = jnp.zeros_like(tail_ref)

    x = x_ref[0]
    h = _rms_norm(x, ng_ref[...]).astype(BF16)
    for j in range(n_blocks):
        z = _dot(h, w_up_ref[j])
        slot = j % 2
        conv = []
        for s in range(2 * gate_slabs):
            ls = slice(s * lanes, (s + 1) * lanes)
            z_ref[slot, s, 0:sub, :] = tail_ref[j, :, ls]
            z_ref[slot, s, sub:sub + rows, :] = z[:, ls]
            w = w_dw_ref[j, :, ls]
            conv.append(w[2:3] * z[:, ls] + w[1:2] * z_ref[slot, s, sub - 1:sub - 1 + rows, :]
                        + w[0:1] * z_ref[slot, s, sub - 2:sub - 2 + rows, :])
        tail_ref[j] = z[rows - sub:]
        for s in range(gate_slabs):
            c0 = j * hb + s * lanes
            act_ref[:, c0:c0 + lanes] = (_silu(conv[s]) * conv[gate_slabs + s]).astype(BF16)
    y = x + _dot(act_ref[...], w_down_ref[...])
    if final_norm:
        y = _rms_norm(y, fg_ref[...])
    o_ref[0] = y


def _layer_call(kernel_fn, x, consts, scratch_shapes, tile, name):
    bsz, seq, d = x.shape
    assert seq % tile == 0
    x_spec = pl.BlockSpec((1, tile, d), lambda b, s: (b, s, 0))

    def const_spec(shape):
        nd = len(shape)
        return pl.BlockSpec(shape, lambda b, s: (0,) * nd, pipeline_mode=pl.Buffered(1))

    return pl.pallas_call(
        kernel_fn,
        grid=(bsz, seq // tile),
        in_specs=[x_spec] + [const_spec(c.shape) for c in consts],
        out_specs=x_spec,
        out_shape=jax.ShapeDtypeStruct(x.shape, x.dtype),
        scratch_shapes=scratch_shapes,
        compiler_params=pltpu.CompilerParams(
            dimension_semantics=("arbitrary", "arbitrary"),
            vmem_limit_bytes=V7X_VMEM_LIMIT_BYTES),
        name=name,
    )(x, *consts)


def _row(v):
    return v.reshape(1, -1)


def _paired_blocks(w, half, blk):
    r = w.shape[0]
    nb = half // blk
    return jnp.concatenate([w[:, :half].reshape(r, nb, blk),
                            w[:, half:].reshape(r, nb, blk)], axis=2).transpose(1, 0, 2)


def _gla_layer(x, norm_g, w_in, w_g2, b_g2, head_norm, w_out):
    heads = GLA_HEADS
    dk_t = w_g2.shape[1]
    dk = dk_t // heads
    dv = head_norm.shape[0]
    dv_t = heads * dv
    rank = w_g2.shape[0]
    w_in_p = jnp.pad(w_in, ((0, 0), (0, V7X_LANES - rank))).astype(BF16)
    w_g2_p = jnp.pad(w_g2, ((0, V7X_LANES - rank), (0, 0))).astype(BF16)
    kern = functools.partial(_gla_kernel, heads=heads, dk=dk, dv=dv, chunk=GLA_CHUNK)
    consts = [_row(norm_g), w_in_p, w_g2_p, _row(b_g2), _row(head_norm), w_out.astype(BF16)]
    scratch = [pltpu.VMEM((heads, dk, dv), F32), pltpu.VMEM((GLA_TILE, dv_t), F32),
               pltpu.VMEM((GLA_TILE, 2 * dv_t), F32)]
    return _layer_call(kern, x, consts, scratch, GLA_TILE, "gla_mixer")


def _conformer_layer(x, norm_g, w_in, b_in, w_dw, b_dw, ln_g, ln_b, w_out, b_out):
    d = x.shape[-1]
    width = w_dw.shape[0]
    lanes = V7X_LANES
    halo = -(-(width - 1) // V7X_SUBLANES) * V7X_SUBLANES
    n_slabs = d // lanes
    w_dw_slabs = w_dw.reshape(width, n_slabs, lanes).transpose(1, 0, 2)
    b_dw_slabs = b_dw.reshape(n_slabs, 1, lanes)
    kern = functools.partial(_conformer_kernel, width=width, halo=halo)
    consts = [_row(norm_g), _paired_blocks(w_in, d, V7X_MXU_WIDTH).astype(BF16),
              _paired_blocks(_row(b_in), d, V7X_MXU_WIDTH), w_dw_slabs, b_dw_slabs, _row(ln_g),
              _row(ln_b), w_out.astype(BF16), _row(b_out)]
    scratch = [pltpu.VMEM((n_slabs, halo + CONFORMER_TILE, lanes), F32),
               pltpu.VMEM((n_slabs, CONFORMER_TILE, lanes), F32)]
    return _layer_call(kern, x, consts, scratch, CONFORMER_TILE, "conformer_mixer")


def _sgu_layer(x, norm_g, w_in, b_in, ln_g, ln_b, w_s, b_s, w_out, b_out):
    d = x.shape[-1]
    groups, chunk, _ = w_s.shape
    b_s_full = jnp.repeat(b_s.T, d // groups, axis=1)
    kern = functools.partial(_sgu_kernel, chunk=chunk, groups=groups)
    consts = [_row(norm_g), w_in.astype(BF16), _row(b_in), _row(ln_g), _row(ln_b), w_s,
              b_s_full, w_out.astype(BF16), _row(b_out)]
    scratch = [pltpu.VMEM((SGU_TILE, d), BF16), pltpu.VMEM((SGU_TILE, d), F32)]
    return _layer_call(kern, x, consts, scratch, SGU_TILE, "sgu_mixer")


def _hgrn2_layer(x, norm_g, w_in, lb_table, head_norm, w_out, layer):
    d = x.shape[-1]
    dk = head_norm.shape[0]
    heads = d // dk
    kern = functools.partial(_hgrn2_kernel, layer=layer, heads=heads, dk=dk,
                             chunk=HGRN_CHUNK)
    consts = [_row(norm_g), w_in.astype(BF16), lb_table, _row(head_norm), w_out.astype(BF16)]
    scratch = [pltpu.VMEM((heads, dk, dk), F32), pltpu.VMEM((HGRN_TILE, d), F32),
               pltpu.VMEM((HGRN_TILE, 2 * d), F32)]
    return _layer_call(kern, x, consts, scratch, HGRN_TILE, "hgrn2_mixer")


def _ffn_layer(x, norm_g, w_up, w_dw, w_down, final_g, final_norm):
    hidden = w_down.shape[0]
    hb = FFN_HIDDEN_BLOCK
    assert hidden % hb == 0
    nb = hidden // hb
    kern = functools.partial(_ffn_kernel, final_norm=final_norm)
    consts = [_row(norm_g), _paired_blocks(w_up, hidden, hb).astype(BF16),
              _paired_blocks(w_dw, hidden, hb), w_down.astype(BF16), _row(final_g)]
    scratch = [pltpu.VMEM((nb, V7X_SUBLANES, 2 * hb), F32), pltpu.VMEM((FFN_TILE, hidden), BF16),
               pltpu.VMEM((2, 2 * hb // V7X_LANES, V7X_SUBLANES + FFN_TILE, V7X_LANES), F32)]
    return _layer_call(kern, x, consts, scratch, FFN_TILE, "conv_ffn")


def kernel(x, norm_mix, norm_ffn, norm_final, gla_w_in, gla_w_g2, gla_b_g2, gla_norm, gla_w_out, cv_w_in, cv_b_in, cv_w_dw, cv_b_dw, cv_ln_g, cv_ln_b, cv_w_out, cv_b_out, sg_w_in, sg_b_in, sg_ln_g, sg_ln_b, sg_w_s, sg_b_s, sg_w_out, sg_b_out, hg_w_in, hg_lb_table, hg_norm, hg_w_out, ffn_w_up, ffn_w_dw, ffn_w_down):
    depth = norm_mix.shape[0]
    for layer in range(depth):
        m, j = layer % N_MIXERS, layer // N_MIXERS
        if m == 0:
            x = _gla_layer(x, norm_mix[layer], gla_w_in[j], gla_w_g2[j], gla_b_g2[j],
                           gla_norm[j], gla_w_out[j])
        elif m == 1:
            x = _conformer_layer(x, norm_mix[layer], cv_w_in[j], cv_b_in[j], cv_w_dw[j],
                                 cv_b_dw[j], cv_ln_g[j], cv_ln_b[j], cv_w_out[j], cv_b_out[j])
        elif m == 2:
            x = _sgu_layer(x, norm_mix[layer], sg_w_in[j], sg_b_in[j], sg_ln_g[j], sg_ln_b[j],
                           sg_w_s[j], sg_b_s[j], sg_w_out[j], sg_b_out[j])
        else:
            x = _hgrn2_layer(x, norm_mix[layer], hg_w_in[j], hg_lb_table, hg_norm[j],
                             hg_w_out[j], layer)
        x = _ffn_layer(x, norm_ffn[layer], ffn_w_up[layer], ffn_w_dw[layer],
                       ffn_w_down[layer], norm_final, final_norm=(layer == depth - 1))
    return x
```

```python
import functools

import jax
import jax.numpy as jnp
from jax import lax
from jax.experimental import pallas as pl
from jax.experimental.pallas import tpu as pltpu

F32 = jnp.float32
BF16 = jnp.bfloat16

EPS = 1e-6
N_MIXERS = 4
GLA_HEADS = 4
GLA_GATE_NORM = 16.0
GLA_CHUNK = 64
HGRN_CHUNK = 64

V7X_LANES = 128
V7X_SUBLANES = 8
V7X_MXU_WIDTH = 256
V7X_VMEM_LIMIT_BYTES = 60 * 1024 * 1024

GLA_TILE = 1024
HGRN_TILE = 1024
CONFORMER_TILE = 1024
SGU_TILE = 1024
FFN_TILE = 1024
PROJ_BLOCK = 2 * V7X_MXU_WIDTH
FFN_HIDDEN_BLOCK = 256


def _dot(a, b):
    return jnp.dot(a.astype(BF16), b.astype(BF16), preferred_element_type=F32)


def _dot_nt(a, b):
    return lax.dot_general(a.astype(BF16), b.astype(BF16),
                           (((1,), (1,)), ((), ())), preferred_element_type=F32)


def _dot_tn(a, b):
    return lax.dot_general(a.astype(BF16), b.astype(BF16),
                           (((0,), (0,)), ((), ())), preferred_element_type=F32)


def _rms_norm(x, g):
    return x * lax.rsqrt(jnp.mean(x * x, axis=-1, keepdims=True) + EPS) * g


def _layer_norm(x, g, b):
    mu = jnp.mean(x, axis=-1, keepdims=True)
    xc = x - mu
    return xc * lax.rsqrt(jnp.mean(xc * xc, axis=-1, keepdims=True) + EPS) * g + b


def _sigmoid(x):
    return 1.0 / (1.0 + jnp.exp(-x))


def _silu(x):
    return x * _sigmoid(x)


def _log_sigmoid(x):
    return jnp.minimum(x, 0.0) - jnp.log1p(jnp.exp(-jnp.abs(x)))


def _gelu_tanh(x):
    c = 0.7978845608028654
    return 0.5 * x * (1.0 + jnp.tanh(c * (x + 0.044715 * (x * x * x))))


def _chunk_cumsum(x, chunk):
    rows = x.shape[0]
    pos = lax.broadcasted_iota(jnp.int32, (rows, 1), 0) & (chunk - 1)
    shift = 1
    while shift < chunk:
        x = x + jnp.where(pos >= shift, pltpu.roll(x, shift, axis=0), 0.0)
        shift *= 2
    return x


def _row_to_col(row):
    n = row.shape[1]
    eye = (lax.broadcasted_iota(jnp.int32, (n, n), 0)
           == lax.broadcasted_iota(jnp.int32, (n, n), 1))
    return jnp.sum(jnp.where(eye, jnp.broadcast_to(row, (n, n)), 0.0),
                   axis=1, keepdims=True)


def _attention_operands(q, k, log_g, chunk):
    rows, width = q.shape
    b3 = _chunk_cumsum(log_g, chunk).reshape(rows // chunk, chunk, width)
    b_mid = b3[:, chunk // 2:chunk // 2 + 1, :]
    b_last = b3[:, chunk - 1:chunk, :]
    q3 = q.reshape(b3.shape)
    k3 = k.reshape(b3.shape)
    q_in = (q3 * jnp.exp(b3 - b_mid)).astype(BF16)
    k_in = (k3 * jnp.exp(b_mid - b3)).astype(BF16)
    q_dec = q3 * jnp.exp(b3)
    k_dec = k3 * jnp.exp(b_last - b3)
    return q_in, k_in, q_dec, k_dec, jnp.exp(b_last)


def _attention_apply(blocks, load_v, state_ref, o_ref, *, heads, dk, dv, chunk, pair_chunks):
    heads_per_block = heads // len(blocks)
    n_chunks = blocks[0][0].shape[0]
    states = [state_ref[h] for h in range(heads)]
    if not pair_chunks:
        causal = (lax.broadcasted_iota(jnp.int32, (chunk, chunk), 0)
                  >= lax.broadcasted_iota(jnp.int32, (chunk, chunk), 1))
        for n in range(n_chunks):
            rs = slice(n * chunk, (n + 1) * chunk)
            for h in range(heads):
                q_in, k_in, q_dec, k_dec, g_chunk = blocks[h // heads_per_block]
                ks = slice((h % heads_per_block) * dk, (h % heads_per_block + 1) * dk)
                vs = slice(h * dv, (h + 1) * dv)
                v_nh = load_v(rs, vs).astype(BF16)
                scores = jnp.where(causal, _dot_nt(q_in[n][:, ks], k_in[n][:, ks]), 0.0)
                o_ref[rs, vs] = _dot(scores, v_nh) + _dot(q_dec[n][:, ks], states[h])
                states[h] = (_row_to_col(g_chunk[n][:, ks]) * states[h]
                             + _dot_tn(k_dec[n][:, ks], v_nh))
    else:
        pair = 2 * chunk
        causal = (lax.broadcasted_iota(jnp.int32, (pair, pair), 0)
                  >= lax.broadcasted_iota(jnp.int32, (pair, pair), 1))
        zeros = jnp.zeros((chunk, dk), BF16)
        for p in range(n_chunks // 2):
            ca, cb = 2 * p, 2 * p + 1
            rs = slice(p * pair, (p + 1) * pair)
            for h in range(heads):
                q_in, k_in, q_dec, k_dec, g_chunk = blocks[h // heads_per_block]
                ks = slice((h % heads_per_block) * dk, (h % heads_per_block + 1) * dk)
                vs = slice(h * dv, (h + 1) * dv)
                g_a, g_b = g_chunk[ca][:, ks], g_chunk[cb][:, ks]
                q_dec_a, q_dec_b = q_dec[ca][:, ks], q_dec[cb][:, ks]
                k_dec_a, k_dec_b = k_dec[ca][:, ks], k_dec[cb][:, ks]
                v_ph = load_v(rs, vs).astype(BF16)
                q_cat = jnp.concatenate([
                    jnp.concatenate([q_in[ca][:, ks], zeros, zeros], axis=1),
                    jnp.concatenate([zeros, q_in[cb][:, ks], q_dec_b.astype(BF16)], axis=1)],
                    axis=0)
                k_cat = jnp.concatenate([
                    jnp.concatenate([k_in[ca][:, ks], zeros, k_dec_a.astype(BF16)], axis=1),
                    jnp.concatenate([zeros, k_in[cb][:, ks], zeros], axis=1)], axis=0)
                scores = jnp.where(causal, _dot_nt(q_cat, k_cat), 0.0).astype(BF16)
                q_state = jnp.concatenate([q_dec_a.astype(BF16), (q_dec_b * g_a).astype(BF16)],
                                          axis=0)
                o_ref[rs, vs] = _dot(
                    jnp.concatenate([scores, q_state], axis=1),
                    jnp.concatenate([v_ph, states[h].astype(BF16)], axis=0))
                k_end = jnp.concatenate([(k_dec_a * g_b).astype(BF16), k_dec_b.astype(BF16)],
                                        axis=0)
                states[h] = _row_to_col(g_a * g_b) * states[h] + _dot_tn(k_end, v_ph)
    for h in range(heads):
        state_ref[h] = states[h]


def _gla_kernel(x_ref, ng_ref, w_in_ref, w_g2_ref, b_g2_ref, hn_ref, w_out_ref, o_ref,
                state_ref, att_ref, vr_ref, *, heads, dk, dv, chunk):
    @pl.when(pl.program_id(1) == 0)
    def _():
        state_ref[...] = jnp.zeros_like(state_ref)

    dk_t, dv_t = heads * dk, heads * dv
    n_main = 2 * dk_t + 2 * dv_t
    x = x_ref[0]
    h = _rms_norm(x, ng_ref[...]).astype(BF16)
    g_lr = _dot(h, w_in_ref[:, n_main:])
    q = _dot(h, w_in_ref[:, :dk_t]) * (dk ** -0.5)
    k = _dot(h, w_in_ref[:, dk_t:2 * dk_t])
    log_g = _log_sigmoid(_dot(g_lr, w_g2_ref[...]) + b_g2_ref[...]) / GLA_GATE_NORM
    blocks = [_attention_operands(q, k, log_g, chunk)]
    vr_ref[...] = _dot(h, w_in_ref[:, 2 * dk_t:n_main])
    _attention_apply(blocks, lambda rs, cs: vr_ref[rs, cs], state_ref, att_ref,
                     heads=heads, dk=dk, dv=dv, chunk=chunk, pair_chunks=True)
    gated = []
    for hd in range(heads):
        vs = slice(hd * dv, (hd + 1) * dv)
        r = vr_ref[:, dv_t + hd * dv:dv_t + (hd + 1) * dv]
        gated.append((_rms_norm(att_ref[:, vs], hn_ref[...]) * _silu(r)).astype(BF16))
    o_ref[0] = x + _dot(jnp.concatenate(gated, axis=1), w_out_ref[...])


def _conformer_kernel(x_ref, ng_ref, w_in_ref, b_in_ref, w_dw_ref, b_dw_ref, ln_g_ref,
                      ln_b_ref, w_out_ref, b_out_ref, o_ref, y_ref, conv_ref, *, width, halo):
    rows = x_ref.shape[1]
    d = x_ref.shape[2]
    lanes, sub = V7X_LANES, V7X_SUBLANES
    n_slabs = d // lanes
    n_blocks, _, two_bw = w_in_ref.shape
    bw = two_bw // 2

    @pl.when(pl.program_id(1) == 0)
    def _():
        y_ref[:, 0:halo, :] = jnp.zeros((n_slabs, halo, lanes), F32)

    @pl.when(pl.program_id(1) != 0)
    def _():
        y_ref[:, 0:halo, :] = y_ref[:, rows:rows + halo, :]

    x = x_ref[0]
    h = _rms_norm(x, ng_ref[...]).astype(BF16)
    for j in range(n_blocks):
        ag = _dot(h, w_in_ref[j]) + b_in_ref[j]
        glu = ag[:, :bw] * _sigmoid(ag[:, bw:])
        for s in range(bw // lanes):
            y_ref[j * (bw // lanes) + s, halo:halo + rows, :] = glu[:, s * lanes:(s + 1) * lanes]

    def conv_slab(c, carry):
        shifted = [y_ref[c, r:r + rows + halo - (sub if r else 0), :] for r in range(sub)]
        acc = jnp.broadcast_to(b_dw_ref[c], (rows, lanes))
        for tap in range(width):
            start = halo - (width - 1) + tap
            r, base = start % sub, start - start % sub
            acc = acc + w_dw_ref[c, tap:tap + 1, :] * shifted[r][base:base + rows]
        conv_ref[c] = acc
        return carry

    lax.fori_loop(0, n_slabs, conv_slab, 0)
    conv = jnp.concatenate([conv_ref[c] for c in range(n_slabs)], axis=1)
    z = _silu(_layer_norm(conv, ln_g_ref[...], ln_b_ref[...]))
    o_ref[0] = x + _dot(z, w_out_ref[...]) + b_out_ref[...]


def _sgu_kernel(x_ref, ng_ref, w_in_ref, b_in_ref, ln_g_ref, ln_b_ref, w_s_ref, b_s_ref,
                w_out_ref, b_out_ref, o_ref, gated_ref, u_ref, *, chunk, groups):
    rows = x_ref.shape[1]
    d = x_ref.shape[2]
    gw = d // groups
    x = x_ref[0]
    h = _rms_norm(x, ng_ref[...]).astype(BF16)

    def gelu_block(c0):
        cs = slice(c0, c0 + PROJ_BLOCK)
        return _gelu_tanh(_dot(h, w_in_ref[:, cs]) + b_in_ref[:, cs])

    v = jnp.concatenate([gelu_block(d + c0) for c0 in range(0, d, PROJ_BLOCK)], axis=1)
    for c0 in range(0, d, PROJ_BLOCK):
        u_ref[:, c0:c0 + PROJ_BLOCK] = gelu_block(c0)
    v = _layer_norm(v, ln_g_ref[...], ln_b_ref[...]).astype(BF16)
    causal = (lax.broadcasted_iota(jnp.int32, (chunk, chunk), 0)
              >= lax.broadcasted_iota(jnp.int32, (chunk, chunk), 1))
    for g in range(groups):
        cs = slice(g * gw, (g + 1) * gw)
        w_causal = jnp.where(causal, w_s_ref[g], 0.0).astype(BF16)
        for n in range(rows // chunk):
            rs = slice(n * chunk, (n + 1) * chunk)
            s = _dot(w_causal, v[rs, cs]) + b_s_ref[:, cs]
            gated_ref[rs, cs] = (u_ref[rs, cs] * s).astype(BF16)
    o_ref[0] = x + _dot(gated_ref[...], w_out_ref[...]) + b_out_ref[...]


def _hgrn2_kernel(x_ref, ng_ref, w_in_ref, lb_table_ref, hn_ref, w_out_ref, o_ref,
                  state_ref, att_ref, vg_ref, *, layer, heads, dk, chunk):
    @pl.when(pl.program_id(1) == 0)
    def _():
        state_ref[...] = jnp.zeros_like(state_ref)

    d = x_ref.shape[2]
    table = lb_table_ref[...]
    e = jnp.exp(table - jnp.max(table, axis=0, keepdims=True))
    lb = jnp.sum(e[1:layer + 1], axis=0, keepdims=True) / jnp.sum(e, axis=0, keepdims=True)

    x = x_ref[0]
    h = _rms_norm(x, ng_ref[...]).astype(BF16)
    blocks = []
    for c0 in range(0, d, PROJ_BLOCK):
        cs = slice(c0, c0 + PROJ_BLOCK)
        q = _silu(_dot(h, w_in_ref[:, cs]))
        sig_f = _sigmoid(_dot(h, w_in_ref[:, d + c0:d + c0 + PROJ_BLOCK]))
        log_f = jnp.log(lb[:, cs] + (1.0 - lb[:, cs]) * sig_f)
        k = (1.0 - lb[:, cs]) * (1.0 - sig_f)
        blocks.append(_attention_operands(q, k, log_f, chunk))
    vg_ref[...] = _dot(h, w_in_ref[:, 2 * d:])
    _attention_apply(blocks, lambda rs, cs: vg_ref[rs, cs], state_ref, att_ref,
                     heads=heads, dk=dk, dv=dk, chunk=chunk, pair_chunks=False)
    gated = []
    for hd in range(heads):
        vs = slice(hd * dk, (hd + 1) * dk)
        gate = vg_ref[:, d + hd * dk:d + (hd + 1) * dk]
        gated.append((_rms_norm(att_ref[:, vs], hn_ref[...]) * _silu(gate)).astype(BF16))
    o_ref[0] = x + _dot(jnp.concatenate(gated, axis=1), w_out_ref[...])


def _ffn_kernel(x_ref, ng_ref, w_up_ref, w_dw_ref, w_down_ref, fg_ref, o_ref,
                tail_ref, act_ref, *, final_norm):
    rows = x_ref.shape[1]
    n_blocks, _, two_hb = w_up_ref.shape
    hb = two_hb // 2
    sub = V7X_SUBLANES

    @pl.when(pl.program_id(1) == 0)
    def _():
        tail_ref[...] = jnp.zeros_like(tail_ref)

    x = x_ref[0]
    h = _rms_norm(x, ng_ref[...]).astype(BF16)

    for j in range(n_blocks):
        z = _dot(h, w_up_ref[j])
        zc = jnp.concatenate([tail_ref[j], z], axis=0)
        tail_ref[j] = z[rows - sub:]
        w = w_dw_ref[j]
        c = (w[2:3] * z + w[1:2] * zc[sub - 1:sub - 1 + rows]
             + w[0:1] * zc[sub - 2:sub - 2 + rows])
        act_ref[:, j * hb:(j + 1) * hb] = (_silu(c[:, :hb]) * c[:, hb:]).astype(BF16)
    y = x + _dot(act_ref[...], w_down_ref[...])
    if final_norm:
        y = _rms_norm(y, fg_ref[...])
    o_ref[0] = y


def _layer_call(kernel_fn, x, consts, scratch_shapes, tile, name):
    bsz, seq, d = x.shape
    assert seq % tile == 0
    x_spec = pl.BlockSpec((1, tile, d), lambda b, s: (b, s, 0))

    def const_spec(shape):
        nd = len(shape)
        return pl.BlockSpec(shape, lambda b, s: (0,) * nd, pipeline_mode=pl.Buffered(1))

    return pl.pallas_call(
        kernel_fn,
        grid=(bsz, seq // tile),
        in_specs=[x_spec] + [const_spec(c.shape) for c in consts],
        out_specs=x_spec,
        out_shape=jax.ShapeDtypeStruct(x.shape, x.dtype),
        scratch_shapes=scratch_shapes,
        compiler_params=pltpu.CompilerParams(
            dimension_semantics=("arbitrary", "arbitrary"),
            vmem_limit_bytes=V7X_VMEM_LIMIT_BYTES),
        name=name,
    )(x, *consts)


def _row(v):
    return v.reshape(1, -1)


def _paired_blocks(w, half, blk):
    r = w.shape[0]
    nb = half // blk
    return jnp.concatenate([w[:, :half].reshape(r, nb, blk),
                            w[:, half:].reshape(r, nb, blk)], axis=2).transpose(1, 0, 2)


def _gla_layer(x, norm_g, w_in, w_g2, b_g2, head_norm, w_out):
    heads = GLA_HEADS
    dk_t = w_g2.shape[1]
    dk = dk_t // heads
    dv = head_norm.shape[0]
    dv_t = heads * dv
    rank = w_g2.shape[0]
    w_in_p = jnp.pad(w_in, ((0, 0), (0, V7X_LANES - rank))).astype(BF16)
    w_g2_p = jnp.pad(w_g2, ((0, V7X_LANES - rank), (0, 0))).astype(BF16)
    kern = functools.partial(_gla_kernel, heads=heads, dk=dk, dv=dv, chunk=GLA_CHUNK)
    consts = [_row(norm_g), w_in_p, w_g2_p, _row(b_g2), _row(head_norm), w_out.astype(BF16)]
    scratch = [pltpu.VMEM((heads, dk, dv), F32), pltpu.VMEM((GLA_TILE, dv_t), F32),
               pltpu.VMEM((GLA_TILE, 2 * dv_t), F32)]
    return _layer_call(kern, x, consts, scratch, GLA_TILE, "gla_mixer")


def _conformer_layer(x, norm_g, w_in, b_in, w_dw, b_dw, ln_g, ln_b, w_out, b_out):
    d = x.shape[-1]
    width = w_dw.shape[0]
    lanes = V7X_LANES
    halo = -(-(width - 1) // V7X_SUBLANES) * V7X_SUBLANES
    n_slabs = d // lanes
    w_dw_slabs = w_dw.reshape(width, n_slabs, lanes).transpose(1, 0, 2)
    b_dw_slabs = b_dw.reshape(n_slabs, 1, lanes)
    kern = functools.partial(_conformer_kernel, width=width, halo=halo)
    consts = [_row(norm_g), _paired_blocks(w_in, d, V7X_MXU_WIDTH).astype(BF16),
              _paired_blocks(_row(b_in), d, V7X_MXU_WIDTH), w_dw_slabs, b_dw_slabs, _row(ln_g),
              _row(ln_b), w_out.astype(BF16), _row(b_out)]
    scratch = [pltpu.VMEM((n_slabs, halo + CONFORMER_TILE, lanes), F32),
               pltpu.VMEM((n_slabs, CONFORMER_TILE, lanes), F32)]
    return _layer_call(kern, x, consts, scratch, CONFORMER_TILE, "conformer_mixer")


def _sgu_layer(x, norm_g, w_in, b_in, ln_g, ln_b, w_s, b_s, w_out, b_out):
    d = x.shape[-1]
    groups, chunk, _ = w_s.shape
    b_s_full = jnp.repeat(b_s.T, d // groups, axis=1)
    kern = functools.partial(_sgu_kernel, chunk=chunk, groups=groups)
    consts = [_row(norm_g), w_in.astype(BF16), _row(b_in), _row(ln_g), _row(ln_b), w_s,
              b_s_full, w_out.astype(BF16), _row(b_out)]
    scratch = [pltpu.VMEM((SGU_TILE, d), BF16), pltpu.VMEM((SGU_TILE, d), F32)]
    return _layer_call(kern, x, consts, scratch, SGU_TILE, "sgu_mixer")


def _hgrn2_layer(x, norm_g, w_in, lb_table, head_norm, w_out, layer):
    d = x.shape[-1]
    dk = head_norm.shape[0]
    heads = d // dk
    kern = functools.partial(_hgrn2_kernel, layer=layer, heads=heads, dk=dk,
                             chunk=HGRN_CHUNK)
    consts = [_row(norm_g), w_in.astype(BF16), lb_table, _row(head_norm), w_out.astype(BF16)]
    scratch = [pltpu.VMEM((heads, dk, dk), F32), pltpu.VMEM((HGRN_TILE, d), F32),
               pltpu.VMEM((HGRN_TILE, 2 * d), F32)]
    return _layer_call(kern, x, consts, scratch, HGRN_TILE, "hgrn2_mixer")


def _ffn_layer(x, norm_g, w_up, w_dw, w_down, final_g, final_norm):
    hidden = w_down.shape[0]
    hb = FFN_HIDDEN_BLOCK
    assert hidden % hb == 0
    nb = hidden // hb
    kern = functools.partial(_ffn_kernel, final_norm=final_norm)
    consts = [_row(norm_g), _paired_blocks(w_up, hidden, hb).astype(BF16),
              _paired_blocks(w_dw, hidden, hb), w_down.astype(BF16), _row(final_g)]
    scratch = [pltpu.VMEM((nb, V7X_SUBLANES, 2 * hb), F32), pltpu.VMEM((FFN_TILE, hidden), BF16)]
    return _layer_call(kern, x, consts, scratch, FFN_TILE, "conv_ffn")


def kernel(x, norm_mix, norm_ffn, norm_final, gla_w_in, gla_w_g2, gla_b_g2, gla_norm, gla_w_out, cv_w_in, cv_b_in, cv_w_dw, cv_b_dw, cv_ln_g, cv_ln_b, cv_w_out, cv_b_out, sg_w_in, sg_b_in, sg_ln_g, sg_ln_b, sg_w_s, sg_b_s, sg_w_out, sg_b_out, hg_w_in, hg_lb_table, hg_norm, hg_w_out, ffn_w_up, ffn_w_dw, ffn_w_down):
    depth = norm_mix.shape[0]
    for layer in range(depth):
        m, j = layer % N_MIXERS, layer // N_MIXERS
        if m == 0:
            x = _gla_layer(x, norm_mix[layer], gla_w_in[j], gla_w_g2[j], gla_b_g2[j],
                           gla_norm[j], gla_w_out[j])
        elif m == 1:
            x = _conformer_layer(x, norm_mix[layer], cv_w_in[j], cv_b_in[j], cv_w_dw[j],
                                 cv_b_dw[j], cv_ln_g[j], cv_ln_b[j], cv_w_out[j], cv_b_out[j])
        elif m == 2:
            x = _sgu_layer(x, norm_mix[layer], sg_w_in[j], sg_b_in[j], sg_ln_g[j], sg_ln_b[j],
                           sg_w_s[j], sg_b_s[j], sg_w_out[j], sg_b_out[j])
        else:
            x = _hgrn2_layer(x, norm_mix[layer], hg_w_in[j], hg_lb_table, hg_norm[j],
                             hg_w_out[j], layer)
        x = _ffn_layer(x, norm_ffn[layer], ffn_w_up[layer], ffn_w_dw[layer],
                       ffn_w_down[layer], norm_final, final_norm=(layer == depth - 1))
    return x
```
